```python
import jax, jax.numpy as jnp
from jax import lax
import numpy as np

D_MODEL = 4096
BATCH = 2
SEQ = 4096
DEPTH = 2

CHUNK = 64
N_META = 16
Q_BLOCK = 128
N_MIXERS = 2
N_DELTA = (DEPTH + N_MIXERS - 1) // N_MIXERS
N_SB = DEPTH // N_MIXERS
EPS = 1e-6

DN_HEAD_K = 128
DN_HEAD_V = 128
DN_HK = D_MODEL // DN_HEAD_K
DN_HV = 2 * DN_HK
DN_KEY = DN_HK * DN_HEAD_K
DN_VAL = DN_HV * DN_HEAD_V
DN_CONV_CH = 2 * DN_KEY + DN_VAL
DN_IN = DN_CONV_CH + DN_VAL + 2 * DN_HV
CONV_K = 4

SB_HEAD = 128
SB_HEADS = D_MODEL // SB_HEAD
SB_WIDTH = SB_HEADS * SB_HEAD

kernel_name = "hybrid_deltanet_stickbreaking_meta"


def rms_norm(x, w):
    xf = x.astype(jnp.float32)
    y = xf * lax.rsqrt(jnp.mean(xf * xf, axis=-1, keepdims=True) + EPS)
    return (y * w.astype(jnp.float32)).astype(x.dtype)


def l2_norm(x):
    xf = x.astype(jnp.float32)
    return xf * lax.rsqrt(jnp.sum(xf * xf, axis=-1, keepdims=True) + EPS)


def causal_depthwise_conv(x, w):
    c = x.shape[-1]
    return lax.conv_general_dilated(
        x, w[:, None, :].astype(x.dtype), window_strides=(1,),
        padding=((CONV_K - 1, 0),), dimension_numbers=("NWC", "WIO", "NWC"),
        feature_group_count=c)


def chunk_gated_delta_rule(q, k, v, g, beta):
    b, h, lc, dk = q.shape
    dv = v.shape[-1]
    n = lc // CHUNK
    f32 = jnp.float32
    q = q.astype(f32) * (dk ** -0.5)
    k = k.astype(f32)
    v = v.astype(f32)
    rs = lambda t: t.reshape(b, h, n, CHUNK, *t.shape[3:])
    q, k, v, g, beta = rs(q), rs(k), rs(v), rs(g.astype(f32)), rs(beta.astype(f32))
    g = jnp.cumsum(g, axis=-1)
    idx = jnp.arange(CHUNK)
    lower_incl = idx[:, None] >= idx[None, :]
    strict = idx[:, None] > idx[None, :]
    decay = jnp.exp(jnp.where(lower_incl, g[..., :, None] - g[..., None, :], -jnp.inf))
    k_beta = k * beta[..., None]
    v_beta = v * beta[..., None]
    l_mat = jnp.where(strict, jnp.einsum("bhnid,bhnjd->bhnij", k_beta, k) * decay, 0.0)
    eye = jnp.eye(CHUNK, dtype=f32)
    t_mat = lax.linalg.triangular_solve(eye + l_mat, jnp.broadcast_to(eye, l_mat.shape),
                                        left_side=True, lower=True)
    u = jnp.einsum("bhnij,bhnjd->bhnid", t_mat, v_beta)
    w = jnp.einsum("bhnij,bhnjd->bhnid", t_mat, k_beta * jnp.exp(g)[..., None])
    attn = jnp.where(lower_incl, jnp.einsum("bhnid,bhnjd->bhnij", q, k) * decay, 0.0)
    q_dec = q * jnp.exp(g)[..., None]
    g_last = g[..., -1]
    k_dec = k * jnp.exp(g_last[..., None] - g)[..., None]

    def step(state, xs):
        u_c, w_c, qd_c, kd_c, a_c, gl_c = xs
        v_new = u_c - jnp.einsum("bhck,bhkv->bhcv", w_c, state)
        out = jnp.einsum("bhck,bhkv->bhcv", qd_c, state) + jnp.einsum("bhij,bhjv->bhiv", a_c, v_new)
        state = state * jnp.exp(gl_c)[..., None, None] + jnp.einsum("bhck,bhcv->bhkv", kd_c, v_new)
        return state, out

    mv = lambda t: jnp.moveaxis(t, 2, 0)
    state0 = jnp.zeros((b, h, dk, dv), f32)
    _, out = lax.scan(step, state0, (mv(u), mv(w), mv(q_dec), mv(k_dec), mv(attn), mv(g_last)))
    return jnp.moveaxis(out, 0, 2).reshape(b, h, lc, dv)


def gated_deltanet_mixer(hn, w_in, conv_w, a_log, dt_bias, out_norm_w, w_out):
    bsz, seq_len, _ = hn.shape
    proj = hn @ w_in
    qkv, z, b_logit, a_logit = jnp.split(
        proj, [DN_CONV_CH, DN_CONV_CH + DN_VAL, DN_CONV_CH + DN_VAL + DN_HV], axis=-1)
    qkv = jax.nn.silu(causal_depthwise_conv(qkv, conv_w))
    q, k, v = jnp.split(qkv, [DN_KEY, 2 * DN_KEY], axis=-1)
    to_heads = lambda t, nh, hd: t.reshape(bsz, seq_len, nh, hd).transpose(0, 2, 1, 3)
    q = jnp.repeat(l2_norm(to_heads(q, DN_HK, DN_HEAD_K)), DN_HV // DN_HK, axis=1)
    k = jnp.repeat(l2_norm(to_heads(k, DN_HK, DN_HEAD_K)), DN_HV // DN_HK, axis=1)
    v = to_heads(v, DN_HV, DN_HEAD_V)
    beta = jax.nn.sigmoid(b_logit.astype(jnp.float32)).transpose(0, 2, 1)
    g = (-jnp.exp(a_log.astype(jnp.float32))
         * jax.nn.softplus(a_logit.astype(jnp.float32) + dt_bias.astype(jnp.float32))).transpose(0, 2, 1)
    pad = CHUNK - N_META
    p4 = lambda t: jnp.pad(t, ((0, 0), (0, 0), (pad, 0), (0, 0)))
    p3 = lambda t: jnp.pad(t, ((0, 0), (0, 0), (pad, 0)))
    o = chunk_gated_delta_rule(p4(q), p4(k), p4(v), p3(g), p3(beta))[:, :, pad:]
    o = rms_norm(o, out_norm_w) * jax.nn.silu(to_heads(z, DN_HV, DN_HEAD_V).astype(jnp.float32))
    o = o.transpose(0, 2, 1, 3).reshape(bsz, seq_len, DN_VAL).astype(hn.dtype)
    return o @ w_out


def stick_breaking_attention(q, k, v):
    lp, d = q.shape[2], q.shape[3]
    scale = d ** -0.5
    outs = []
    for i in range(lp // Q_BLOCK):
        t0, t1 = i * Q_BLOCK, (i + 1) * Q_BLOCK
        z = jnp.einsum("bhtd,bhsd->bhts", q[:, :, t0:t1], k[:, :, :t1]).astype(jnp.float32) * scale
        causal = jnp.arange(t1)[None, :] < jnp.arange(t0, t1)[:, None]
        log_keep = jnp.where(causal, jax.nn.log_sigmoid(-z), 0.0)
        tail = lax.cumsum(log_keep, axis=3, reverse=True) - log_keep
        a = jnp.where(causal, jnp.exp(jax.nn.log_sigmoid(z) + tail), 0.0)
        outs.append(jnp.einsum("bhts,bhsd->bhtd", a.astype(v.dtype), v[:, :, :t1]))
    return jnp.concatenate(outs, axis=2)


def stick_breaking_mixer(hn, w_in, q_norm_w, k_norm_w, w_out):
    bsz, seq_len, _ = hn.shape
    q, k, v, gate = jnp.split(hn @ w_in, 4, axis=-1)
    to_heads = lambda t: t.reshape(bsz, seq_len, SB_HEADS, SB_HEAD).transpose(0, 2, 1, 3)
    q = rms_norm(to_heads(q), q_norm_w)
    k = rms_norm(to_heads(k), k_norm_w)
    v = to_heads(v)
    lp = -(-seq_len // Q_BLOCK) * Q_BLOCK
    padr = lambda t: jnp.pad(t, ((0, 0), (0, 0), (0, lp - seq_len), (0, 0)))
    o = stick_breaking_attention(padr(q), padr(k), padr(v))[:, :, :seq_len]
    o = o.transpose(0, 2, 1, 3).reshape(bsz, seq_len, SB_WIDTH) * jax.nn.silu(gate)
    return o @ w_out


def setup_inputs(seed: int = 0) -> dict:
    key = jax.random.key(seed)
    ks = jax.random.split(key, 16)
    f32 = jnp.float32
    nrm = lambda k, shape, scale: jax.random.normal(k, shape, f32) * scale
    dt = jnp.exp(jax.random.uniform(ks[6], (N_DELTA, DN_HV), f32, np.log(1e-3), np.log(1e-1)))
    return {
        "x": nrm(ks[0], (BATCH, SEQ, D_MODEL), 1.0),
        "meta_tokens": nrm(ks[1], (N_META, D_MODEL), 1.0),
        "dn_norm_w": 1.0 + nrm(ks[2], (N_DELTA, D_MODEL), 0.02),
        "dn_w_in": nrm(ks[3], (N_DELTA, D_MODEL, DN_IN), D_MODEL ** -0.5),
        "dn_conv_w": nrm(ks[4], (N_DELTA, CONV_K, DN_CONV_CH), CONV_K ** -0.5),
        "dn_a_log": jnp.log(jax.random.uniform(ks[5], (N_DELTA, DN_HV), f32, 1.0, 16.0)),
        "dn_dt_bias": dt + jnp.log(-jnp.expm1(-dt)),
        "dn_out_norm_w": 1.0 + nrm(ks[7], (N_DELTA, DN_HEAD_V), 0.02),
        "dn_w_out": nrm(ks[8], (N_DELTA, DN_VAL, D_MODEL), DN_VAL ** -0.5),
        "sb_norm_w": 1.0 + nrm(ks[9], (N_SB, D_MODEL), 0.02),
        "sb_w_in": nrm(ks[10], (N_SB, D_MODEL, 4 * SB_WIDTH), D_MODEL ** -0.5),
        "sb_q_norm_w": 1.0 + nrm(ks[11], (N_SB, SB_HEAD), 0.02),
        "sb_k_norm_w": 1.0 + nrm(ks[12], (N_SB, SB_HEAD), 0.02),
        "sb_w_out": nrm(ks[13], (N_SB, SB_WIDTH, D_MODEL), SB_WIDTH ** -0.5),
    }


def reference(x, meta_tokens, dn_norm_w, dn_w_in, dn_conv_w, dn_a_log, dn_dt_bias,
              dn_out_norm_w, dn_w_out, sb_norm_w, sb_w_in, sb_q_norm_w, sb_k_norm_w, sb_w_out):
    bsz = x.shape[0]
    meta = jnp.broadcast_to(meta_tokens[None].astype(x.dtype), (bsz, N_META, D_MODEL))
    h = jnp.concatenate([meta, x], axis=1)
    for i in range(DEPTH):
        j = i // N_MIXERS
        if i % N_MIXERS == 0:
            h = h + gated_deltanet_mixer(rms_norm(h, dn_norm_w[j]), dn_w_in[j], dn_conv_w[j],
                                         dn_a_log[j], dn_dt_bias[j], dn_out_norm_w[j], dn_w_out[j])
        else:
            h = h + stick_breaking_mixer(rms_norm(h, sb_norm_w[j]), sb_w_in[j],
                                         sb_q_norm_w[j], sb_k_norm_w[j], sb_w_out[j])
    return h[:, N_META:]
```

```python
import functools

import jax
import jax.numpy as jnp
from jax import lax
from jax.experimental import pallas as pl
from jax.experimental.pallas import tpu as pltpu

F32 = jnp.float32
BF16 = jnp.bfloat16

LANES = 128
CHUNK = 64
N_META = 16
PAD = CHUNK - N_META
CONV_K = 4
HEAD = 128
EPS = 1e-6
Q_TILE = 128
K_TILE = 256
VMEM_LIMIT = 56 * 1024 * 1024

NT_DIMS = (((1,), (1,)), ((), ()))
TN_DIMS = (((0,), (0,)), ((), ()))


def _round_up(x, m):
    return -(-x // m) * m


def _dot(a, b):
    return jnp.dot(a, b, preferred_element_type=F32)


def _rmsnorm_kernel(x_ref, w_ref, o_ref):
    x = x_ref[...]
    ms = jnp.mean(x * x, axis=-1, keepdims=True)
    o_ref[...] = (x * lax.rsqrt(ms + EPS) * w_ref[...]).astype(o_ref.dtype)


def _rmsnorm(x, w, tm):
    m, d = x.shape
    return pl.pallas_call(
        _rmsnorm_kernel,
        grid=(m // tm,),
        in_specs=[pl.BlockSpec((tm, d), lambda i: (i, 0)),
                  pl.BlockSpec((1, d), lambda i: (0, 0))],
        out_specs=pl.BlockSpec((tm, d), lambda i: (i, 0)),
        out_shape=jax.ShapeDtypeStruct((m, d), BF16),
        compiler_params=pltpu.CompilerParams(dimension_semantics=("parallel",),
                                             vmem_limit_bytes=VMEM_LIMIT),
        name="rmsnorm",
    )(x, w.reshape(1, d).astype(F32))


def _mm_kernel(*refs, nk, head_major, has_res):
    if has_res:
        a_ref, w_ref, r_ref, o_ref = refs[:4]
        acc_ref = refs[4] if nk > 1 else None
    else:
        a_ref, w_ref, o_ref = refs[:3]
        r_ref = None
        acc_ref = refs[3] if nk > 1 else None

    def finish(res):
        if has_res:
            res = res + r_ref[...]
        if head_major:
            for g in range(o_ref.shape[0]):
                o_ref[g] = res[:, g * LANES:(g + 1) * LANES].astype(o_ref.dtype)
        else:
            o_ref[...] = res.astype(o_ref.dtype)

    part = _dot(a_ref[...], w_ref[...])
    if nk == 1:
        finish(part)
    else:
        k = pl.program_id(2)

        @pl.when(k == 0)
        def _():
            acc_ref[...] = part

        @pl.when(jnp.logical_and(k > 0, k < nk - 1))
        def _():
            acc_ref[...] += part

        @pl.when(k == nk - 1)
        def _():
            finish(acc_ref[...] + part)


def _matmul(a, w, *, tm, tn, tk, head_major=False, res=None, out_dtype=F32, name="matmul"):
    m, kdim = a.shape
    n = w.shape[1]
    nk = kdim // tk
    in_specs = [pl.BlockSpec((tm, tk), lambda i, j, k: (i, k)),
                pl.BlockSpec((tk, tn), lambda i, j, k: (k, j))]
    args = [a, w]
    if res is not None:
        in_specs.append(pl.BlockSpec((tm, tn), lambda i, j, k: (i, j)))
        args.append(res)
    if head_major:
        out_spec = pl.BlockSpec((tn // LANES, tm, LANES), lambda i, j, k: (j, i, 0))
        out_shape = jax.ShapeDtypeStruct((n // LANES, m, LANES), out_dtype)
    else:
        out_spec = pl.BlockSpec((tm, tn), lambda i, j, k: (i, j))
        out_shape = jax.ShapeDtypeStruct((m, n), out_dtype)
    scratch = [pltpu.VMEM((tm, tn), F32)] if nk > 1 else []
    return pl.pallas_call(
        functools.partial(_mm_kernel, nk=nk, head_major=head_major, has_res=res is not None),
        grid=(m // tm, n // tn, nk),
        in_specs=in_specs,
        out_specs=out_spec,
        out_shape=out_shape,
        scratch_shapes=scratch,
        compiler_params=pltpu.CompilerParams(
            dimension_semantics=("parallel", "parallel", "arbitrary"),
            vmem_limit_bytes=VMEM_LIMIT),
        name=name,
    )(*args)


def _gate_kernel(ba_ref, alog_ref, dtb_ref, md_ref, mb_ref, rows_ref, *, npair):
    c = pl.program_id(1)
    half = LANES // 2
    lane = lax.broadcasted_iota(jnp.int32, (CHUNK, LANES), 1)
    sub = lax.broadcasted_iota(jnp.int32, (CHUNK, LANES), 0)
    is_b = lane < half
    jloc = jnp.where(is_b, lane, lane - half)
    low_incl = sub >= jloc
    low_strict = sub > jloc
    tri = (lax.broadcasted_iota(jnp.int32, (CHUNK, CHUNK), 0)
           >= lax.broadcasted_iota(jnp.int32, (CHUNK, CHUNK), 1)).astype(F32)
    neg_a = -jnp.exp(alog_ref[...])
    dtb = dtb_ref[...]

    def one_chunk(x, first):
        beta = jax.nn.sigmoid(x)
        g = neg_a * jax.nn.softplus(x + dtb)
        if first:
            real = sub >= PAD
            beta = jnp.where(real, beta, 0.0)
            g = jnp.where(real, g, 0.0)
        g = jnp.where(is_b, 0.0, g)
        gc = jnp.dot(tri, g, preferred_element_type=F32, precision=lax.Precision.HIGHEST)
        m1 = jnp.where(is_b, beta, gc)
        glb = jnp.broadcast_to(gc[CHUNK - 1:CHUNK, :], (CHUNK, LANES))
        zpad = jnp.zeros((LANES - CHUNK, LANES), F32)
        m1t = jnp.concatenate([m1, zpad], axis=0).T
        glt = jnp.concatenate([glb, zpad], axis=0).T
        q = npair
        beta_p = jnp.concatenate([m1t[0:q, :half], m1t[q:2 * q, :half]], axis=1)
        gc_p = jnp.concatenate([m1t[2 * q:3 * q, :half], m1t[3 * q:4 * q, :half]], axis=1)
        gl_p = jnp.concatenate([glt[2 * q:3 * q, :half], glt[3 * q:4 * q, :half]], axis=1)
        for p in range(npair):
            rows_ref[0, 0, p, pl.ds(c, 1), :] = beta_p[p:p + 1, :]
            rows_ref[0, 1, p, pl.ds(c, 1), :] = gc_p[p:p + 1, :]
            rows_ref[0, 2, p, pl.ds(c, 1), :] = gl_p[p:p + 1, :]
            gcol = jnp.where(is_b, gc[:, 2 * q + p:2 * q + p + 1], gc[:, 3 * q + p:3 * q + p + 1])
            bcol = jnp.where(is_b, beta[:, p:p + 1], beta[:, q + p:q + p + 1])
            dec = jnp.exp(gcol - gc_p[p:p + 1, :])
            md_ref[0, 0, p] = jnp.where(low_incl, dec, 0.0)
            mb_ref[0, 0, p] = jnp.where(low_strict, dec * bcol, 0.0)

    @pl.when(c == 0)
    def _():
        rows_ref[...] = jnp.zeros(rows_ref.shape, F32)
        one_chunk(jnp.concatenate([jnp.zeros((PAD, LANES), F32), ba_ref[0:N_META, :]], axis=0), True)

    @pl.when(c > 0)
    def _():
        r0 = pl.multiple_of(c * CHUNK - PAD, 16)
        one_chunk(ba_ref[pl.ds(r0, CHUNK), :], False)


def _gate_prep(ba, alog, dtb, *, bsz, lp, nc):
    npair = LANES // 4
    ncp = _round_up(nc, 8)
    big = jax.ShapeDtypeStruct((bsz, nc, npair, CHUNK, LANES), F32)
    return pl.pallas_call(
        functools.partial(_gate_kernel, npair=npair),
        grid=(bsz, nc),
        in_specs=[pl.BlockSpec((lp, LANES), lambda b, c: (b, 0)),
                  pl.BlockSpec((1, LANES), lambda b, c: (0, 0)),
                  pl.BlockSpec((1, LANES), lambda b, c: (0, 0))],
        out_specs=[pl.BlockSpec((1, 1, npair, CHUNK, LANES), lambda b, c: (b, c, 0, 0, 0)),
                   pl.BlockSpec((1, 1, npair, CHUNK, LANES), lambda b, c: (b, c, 0, 0, 0)),
                   pl.BlockSpec((1, 3, npair, ncp, LANES), lambda b, c: (b, 0, 0, 0, 0))],
        out_shape=[big, big, jax.ShapeDtypeStruct((bsz, 3, npair, ncp, LANES), F32)],
        compiler_params=pltpu.CompilerParams(dimension_semantics=("parallel", "arbitrary"),
                                             vmem_limit_bytes=VMEM_LIMIT),
        name="gate_prep",
    )(ba, alog, dtb)


def _delta_kernel(q_ref, k_ref, v_ref, z_ref, cwq_ref, cwk_ref, cwv_ref, onw_ref,
                  md_ref, mb_ref, rows_ref, o_ref,
                  qs, ks, vs, ob, wq_s, u_s, att_s, kd_s, st_s, *, lp, nc, rb):
    half = LANES // 2
    ls = qs.shape[0]
    nblk = lp // rb

    ob[0:16, 0:LANES] = jnp.zeros((16, LANES), F32)

    def conv_slab(load, w, store, normalize, scale):
        for blk in range(nblk):
            t0 = blk * rb
            ob[16 + t0:16 + t0 + rb, 0:LANES] = load(t0).astype(F32)
        for blk in range(nblk):
            t0 = blk * rb
            y = w[CONV_K - 1:CONV_K, :] * ob[16 + t0:16 + t0 + rb, 0:LANES]
            for j in range(CONV_K - 1):
                off = 16 + t0 - (CONV_K - 1) + j
                y = y + w[j:j + 1, :] * ob[off:off + rb, 0:LANES]
            y = y * jax.nn.sigmoid(y)
            if normalize:
                ss = jnp.sum(y * y, axis=-1, keepdims=True)
                y = y * (lax.rsqrt(ss + EPS) * scale)
            store(t0, y.astype(BF16))

    def store_to(ref):
        def f(t0, y):
            ref[PAD + t0:PAD + t0 + rb, :] = y
        return f

    def store_v(h):
        def f(t0, y):
            vs[h, PAD + t0:PAD + t0 + rb, :] = y
        return f

    zero_front = jnp.zeros((PAD, LANES), BF16)
    qs[0:PAD, :] = zero_front
    ks[0:PAD, :] = zero_front
    vs[0, 0:PAD, :] = zero_front
    vs[1, 0:PAD, :] = zero_front
    if ls > PAD + lp:
        tail = jnp.zeros((ls - PAD - lp, LANES), BF16)
        qs[PAD + lp:ls, :] = tail
        ks[PAD + lp:ls, :] = tail
        vs[0, PAD + lp:ls, :] = tail
        vs[1, PAD + lp:ls, :] = tail
    conv_slab(lambda t0: q_ref[0, t0:t0 + rb, :], cwq_ref[0], store_to(qs), True, HEAD ** -0.5)
    conv_slab(lambda t0: k_ref[0, t0:t0 + rb, :], cwk_ref[0], store_to(ks), True, 1.0)
    conv_slab(lambda t0: v_ref[0, t0:t0 + rb, :], cwv_ref[0], store_v(0), False, 1.0)
    conv_slab(lambda t0: v_ref[1, t0:t0 + rb, :], cwv_ref[1], store_v(1), False, 1.0)

    lane = lax.broadcasted_iota(jnp.int32, (CHUNK, LANES), 1)
    sub = lax.broadcasted_iota(jnp.int32, (CHUNK, LANES), 0)
    left = lane < half
    eye2 = (jnp.where(left, lane, lane - half) == sub).astype(F32)
    zero_b = jnp.zeros((CHUNK, LANES), BF16)
    zero_f = jnp.zeros((CHUNK, LANES), F32)

    def blockdiag(p):
        return jnp.concatenate([jnp.where(left, p, zero_f), jnp.where(left, zero_f, p)],
                               axis=0).astype(BF16)

    def blockdiag_wide(a, b):
        return jnp.concatenate([jnp.concatenate([a, zero_b], axis=1),
                                jnp.concatenate([zero_b, b], axis=1)], axis=0)

    def local(c, carry):
        r0 = pl.multiple_of(c * CHUNK, CHUNK)
        kb = ks[pl.ds(r0, CHUNK), :]
        qb = qs[pl.ds(r0, CHUNK), :]
        a2 = lax.dot_general(jnp.concatenate([kb, qb], axis=0), jnp.concatenate([kb, kb], axis=0),
                             NT_DIMS, preferred_element_type=F32)
        n = -(a2[:CHUNK] * mb_ref[0, c, 0])
        att = a2[CHUNK:] * md_ref[0, c, 0]
        p = _dot(n.astype(BF16), blockdiag(n))
        t = eye2 + n
        for _ in range(4):
            r = _dot(jnp.concatenate([p, t], axis=0).astype(BF16), blockdiag(p))
            p = r[:CHUNK]
            t = t + r[CHUNK:]
        t = t + _dot(t.astype(BF16), blockdiag(p))

        beta_p = rows_ref[0, 0, 0, pl.ds(c, 1), :]
        gc_p = rows_ref[0, 1, 0, pl.ds(c, 1), :]
        gl_p = rows_ref[0, 2, 0, pl.ds(c, 1), :]
        egc = jnp.exp(gc_p)
        ekd = jnp.exp(gl_p - gc_p)
        v0 = vs[0, pl.ds(r0, CHUNK), :]
        v1 = vs[1, pl.ds(r0, CHUNK), :]
        u2 = _dot((t * beta_p).astype(BF16), blockdiag_wide(v0, v1))
        wk = _dot(jnp.concatenate([(t * (beta_p * egc)).astype(BF16),
                                   (eye2 * ekd).astype(BF16)], axis=0),
                  blockdiag_wide(kb, kb))
        qd2 = _dot((eye2 * egc).astype(BF16), blockdiag_wide(qb, qb))
        for h in range(2):
            sl = slice(h * LANES, (h + 1) * LANES)
            wq_s[c, h] = jnp.concatenate([wk[:CHUNK, sl], qd2[:, sl]], axis=0).astype(BF16)
        u_s[c] = u2
        att_s[c] = att.astype(BF16)
        kd_s[c] = wk[CHUNK:].astype(BF16)
        return carry

    lax.fori_loop(0, nc, local, 0)

    st_s[...] = jnp.zeros(st_s.shape, F32)
    if ls > nc * CHUNK:
        ob[nc * CHUNK:ls, :] = jnp.zeros((ls - nc * CHUNK, 2 * LANES), F32)
    left8 = lax.broadcasted_iota(jnp.int32, (8, LANES), 1) < half

    def state(c, carry):
        r0 = pl.multiple_of(c * CHUNK, CHUNK)
        egl = jnp.broadcast_to(jnp.exp(rows_ref[0, 2, 0, pl.ds(c, 1), :]), (8, LANES))
        egl_sw = pltpu.roll(egl, half, 1)
        egl_h = (jnp.where(left8, egl, egl_sw)[0:1], jnp.where(left8, egl_sw, egl)[0:1])
        u2 = u_s[c]
        rs, vn = [], []
        for h in range(2):
            r = _dot(wq_s[c, h], st_s[h].astype(BF16))
            rs.append(r)
            vn.append((u2[:, h * LANES:(h + 1) * LANES] - r[:CHUNK]).astype(BF16))
        o2 = _dot(att_s[c], blockdiag_wide(vn[0], vn[1]))
        ob[pl.ds(r0, CHUNK), :] = o2 + jnp.concatenate([rs[0][CHUNK:], rs[1][CHUNK:]], axis=1)
        kd2 = kd_s[c]
        for h in range(2):
            upd = lax.dot_general(kd2[:, h * LANES:(h + 1) * LANES], vn[h], TN_DIMS,
                                  preferred_element_type=F32)
            st_s[h] = st_s[h] * egl_h[h] + upd
        return carry

    lax.fori_loop(0, nc, state, 0)

    onw = onw_ref[...]
    for blk in range(nblk):
        t0 = blk * rb
        for h in range(2):
            o = ob[PAD + t0:PAD + t0 + rb, h * LANES:(h + 1) * LANES]
            ms = jnp.mean(o * o, axis=-1, keepdims=True)
            zz = z_ref[h, t0:t0 + rb, :].astype(F32)
            o_ref[t0:t0 + rb, h * LANES:(h + 1) * LANES] = (
                o * lax.rsqrt(ms + EPS) * onw * (zz * jax.nn.sigmoid(zz))).astype(o_ref.dtype)


def _delta_rule(pj, cw, onw, md, mb, rows, *, bsz, lp, nc, hk_n):
    ls = max(PAD + lp, nc * CHUNK)
    rb = lp // 4 if lp % 32 == 0 and lp >= 1024 else lp
    kern = functools.partial(_delta_kernel, lp=lp, nc=nc, rb=rb)
    return pl.pallas_call(
        kern,
        grid=(bsz, hk_n),
        in_specs=[
            pl.BlockSpec((1, lp, LANES), lambda b, h: (h, b, 0)),
            pl.BlockSpec((1, lp, LANES), lambda b, h: (hk_n + h, b, 0)),
            pl.BlockSpec((2, lp, LANES), lambda b, h: (hk_n + h, b, 0)),
            pl.BlockSpec((2, lp, LANES), lambda b, h: (2 * hk_n + h, b, 0)),
            pl.BlockSpec((1, CONV_K, LANES), lambda b, h: (h, 0, 0)),
            pl.BlockSpec((1, CONV_K, LANES), lambda b, h: (hk_n + h, 0, 0)),
            pl.BlockSpec((2, CONV_K, LANES), lambda b, h: (hk_n + h, 0, 0)),
            pl.BlockSpec((1, LANES), lambda b, h: (0, 0)),
            pl.BlockSpec((1, nc, 1, CHUNK, LANES), lambda b, h: (b, 0, h, 0, 0)),
            pl.BlockSpec((1, nc, 1, CHUNK, LANES), lambda b, h: (b, 0, h, 0, 0)),
            pl.BlockSpec((1, 3, 1, _round_up(nc, 8), LANES), lambda b, h: (b, 0, h, 0, 0)),
        ],
        out_specs=pl.BlockSpec((lp, 2 * LANES), lambda b, h: (b, h)),
        out_shape=jax.ShapeDtypeStruct((bsz * lp, 2 * hk_n * LANES), BF16),
        scratch_shapes=[
            pltpu.VMEM((ls, LANES), BF16),
            pltpu.VMEM((ls, LANES), BF16),
            pltpu.VMEM((2, ls, LANES), BF16),
            pltpu.VMEM((ls, 2 * LANES), F32),
            pltpu.VMEM((nc, 2, 2 * CHUNK, LANES), BF16),
            pltpu.VMEM((nc, CHUNK, 2 * LANES), F32),
            pltpu.VMEM((nc, CHUNK, LANES), BF16),
            pltpu.VMEM((nc, CHUNK, 2 * LANES), BF16),
            pltpu.VMEM((2, HEAD, HEAD), F32),
        ],
        compiler_params=pltpu.CompilerParams(dimension_semantics=("parallel", "arbitrary"),
                                             vmem_limit_bytes=VMEM_LIMIT),
        name="delta_rule",
    )(pj, pj, pj, pj, cw, cw, cw, onw, md, mb, rows)


def _attn_kernel(q_ref, k_ref, v_ref, g_ref, qw_ref, kw_ref, o_ref, qn, kn, vn, *, lp, rb):
    lk = kn.shape[0]
    nblk = lp // rb
    scale = HEAD ** -0.5
    if lk > lp:
        kn[lp:lk, :] = jnp.zeros((lk - lp, HEAD), BF16)
        vn[lp:lk, :] = jnp.zeros((lk - lp, HEAD), BF16)
    qw = qw_ref[...]
    kw = kw_ref[...]
    for blk in range(nblk):
        t0 = blk * rb
        x = q_ref[0, t0:t0 + rb, :].astype(F32)
        ms = jnp.mean(x * x, axis=-1, keepdims=True)
        qn[t0:t0 + rb, :] = (x * lax.rsqrt(ms + EPS) * (qw * scale)).astype(BF16)
        x = k_ref[0, t0:t0 + rb, :].astype(F32)
        ms = jnp.mean(x * x, axis=-1, keepdims=True)
        kn[t0:t0 + rb, :] = (x * lax.rsqrt(ms + EPS) * kw).astype(BF16)
        vn[t0:t0 + rb, :] = v_ref[0, t0:t0 + rb, :]

    upper = (lax.broadcasted_iota(jnp.int32, (K_TILE, K_TILE), 0)
             > lax.broadcasted_iota(jnp.int32, (K_TILE, K_TILE), 1)).astype(BF16)
    row_i = lax.broadcasted_iota(jnp.int32, (Q_TILE, K_TILE), 0)
    col_i = lax.broadcasted_iota(jnp.int32, (Q_TILE, K_TILE), 1)

    def q_loop(qi, carry):
        q0 = pl.multiple_of(qi * Q_TILE, Q_TILE)
        q = qn[pl.ds(q0, Q_TILE), :]
        t_idx = q0 + row_i
        nkt = qi // 2 + 1

        def k_loop(j, c2):
            acc, cr = c2
            k0 = pl.multiple_of((nkt - 1 - j) * K_TILE, K_TILE)
            z = lax.dot_general(q, kn[pl.ds(k0, K_TILE), :], NT_DIMS, preferred_element_type=F32)
            causal = (k0 + col_i) < t_idx
            sp = jnp.maximum(z, 0.0) + jnp.log(1.0 + jnp.exp(-jnp.abs(z)))
            lkeep = jnp.where(causal, -sp, 0.0)
            hi = lkeep.astype(BF16)
            lo = (lkeep - hi.astype(F32)).astype(BF16)
            tl = _dot(hi, upper) + _dot(lo, upper)
            a = jnp.where(causal, jnp.exp(z - sp + tl + cr), 0.0)
            acc = acc + _dot(a.astype(BF16), vn[pl.ds(k0, K_TILE), :])
            cr = cr + tl[:, 0:1] + lkeep[:, 0:1]
            return acc, cr

        acc, _ = lax.fori_loop(0, nkt, k_loop,
                               (jnp.zeros((Q_TILE, HEAD), F32), jnp.zeros((Q_TILE, 1), F32)))
        g = g_ref[0, pl.ds(q0, Q_TILE), :].astype(F32)
        o_ref[pl.ds(q0, Q_TILE), :] = (acc * (g * jax.nn.sigmoid(g))).astype(o_ref.dtype)
        return carry

    lax.fori_loop(0, lp // Q_TILE, q_loop, 0)


def _sb_attention(pj, qw, kw, *, bsz, lp, nh):
    lk = _round_up(lp, K_TILE)
    rb = lp // 4 if lp % 64 == 0 and lp >= 1024 else lp
    return pl.pallas_call(
        functools.partial(_attn_kernel, lp=lp, rb=rb),
        grid=(bsz, nh),
        in_specs=[
            pl.BlockSpec((1, lp, HEAD), lambda b, h: (h, b, 0)),
            pl.BlockSpec((1, lp, HEAD), lambda b, h: (nh + h, b, 0)),
            pl.BlockSpec((1, lp, HEAD), lambda b, h: (2 * nh + h, b, 0)),
            pl.BlockSpec((1, lp, HEAD), lambda b, h: (3 * nh + h, b, 0)),
            pl.BlockSpec((1, HEAD), lambda b, h: (0, 0)),
            pl.BlockSpec((1, HEAD), lambda b, h: (0, 0)),
        ],
        out_specs=pl.BlockSpec((lp, HEAD), lambda b, h: (b, h)),
        out_shape=jax.ShapeDtypeStruct((bsz * lp, nh * HEAD), BF16),
        scratch_shapes=[pltpu.VMEM((lp, HEAD), BF16),
                        pltpu.VMEM((lk, HEAD), BF16),
                        pltpu.VMEM((lk, HEAD), BF16)],
        compiler_params=pltpu.CompilerParams(dimension_semantics=("parallel", "parallel"),
                                             vmem_limit_bytes=VMEM_LIMIT),
        name="sb_attention",
    )(pj, pj, pj, pj, qw, kw)


def _pick_tile(total, prefs):
    for t in prefs:
        if total % t == 0:
            return t
    return total


def kernel(x, meta_tokens, dn_norm_w, dn_w_in, dn_conv_w, dn_a_log, dn_dt_bias, dn_out_norm_w, dn_w_out,
           sb_norm_w, sb_w_in, sb_q_norm_w, sb_k_norm_w, sb_w_out):
    bsz, seq, d = x.shape
    hk_n = d // HEAD
    hv_n = 2 * hk_n
    assert hv_n == LANES // 2, "gate layout assumes 64 value heads"
    l_real = N_META + seq
    lp = _round_up(l_real, LANES)
    nc = -(-(PAD + l_real) // CHUNK)
    m = bsz * lp
    tm = _pick_tile(m, (768, 512, 384, 256, 128))
    tn_rms = _pick_tile(m, (256, 128))

    meta = jnp.broadcast_to(meta_tokens[None].astype(x.dtype), (bsz, N_META, d))
    h0 = jnp.concatenate([meta, x, jnp.zeros((bsz, lp - l_real, d), x.dtype)], axis=1).reshape(m, d)

    n_main = 2 * hk_n * HEAD + 2 * hv_n * HEAD
    w_main = dn_w_in[0, :, :n_main].astype(BF16)
    perm = jnp.concatenate([jnp.arange(0, hv_n, 2), jnp.arange(1, hv_n, 2)])
    perm = jnp.concatenate([perm, hv_n + perm])
    w_ba = dn_w_in[0, :, n_main:][:, perm].astype(BF16)
    half_perm = perm[:hv_n]
    zeros64 = jnp.zeros((hv_n,), F32)
    alog = jnp.concatenate([zeros64, dn_a_log[0][half_perm].astype(F32)]).reshape(1, LANES)
    dtb = jnp.concatenate([zeros64, dn_dt_bias[0][half_perm].astype(F32)]).reshape(1, LANES)

    hn = _rmsnorm(h0, dn_norm_w[0], tn_rms)
    pj0 = _matmul(hn, w_main, tm=tm, tn=1024, tk=d, head_major=True, out_dtype=BF16, name="dn_in_proj")
    ba = _matmul(hn, w_ba, tm=tm, tn=LANES, tk=d, name="dn_gate_proj")
    md, mb, rows = _gate_prep(ba, alog, dtb, bsz=bsz, lp=lp, nc=nc)
    cw = dn_conv_w[0].astype(F32).reshape(CONV_K, -1, LANES).transpose(1, 0, 2)
    o0 = _delta_rule(pj0, cw, dn_out_norm_w[0].astype(F32).reshape(1, LANES), md, mb, rows,
                     bsz=bsz, lp=lp, nc=nc, hk_n=hk_n)
    h1 = _matmul(o0, dn_w_out[0].astype(BF16), tm=tm, tn=512, tk=2 * d, res=h0, name="dn_out_proj")

    hn1 = _rmsnorm(h1, sb_norm_w[0], tn_rms)
    pj1 = _matmul(hn1, sb_w_in[0].astype(BF16), tm=tm, tn=1024, tk=d, head_major=True, out_dtype=BF16,
                  name="sb_in_proj")
    o1 = _sb_attention(pj1, sb_q_norm_w[0].astype(F32).reshape(1, HEAD),
                       sb_k_norm_w[0].astype(F32).reshape(1, HEAD), bsz=bsz, lp=lp, nh=hk_n)
    h2 = _matmul(o1, sb_w_out[0].astype(BF16), tm=tm, tn=1024, tk=d, res=h1, name="sb_out_proj")
    return h2.reshape(bsz, lp, d)[:, N_META:l_real]
```

```python
import functools

import jax
import jax.numpy as jnp
from jax import lax
from jax.experimental import pallas as pl
from jax.experimental.pallas import tpu as pltpu

F32 = jnp.float32
BF16 = jnp.bfloat16

LANES = 128
CHUNK = 64
N_META = 16
PAD = CHUNK - N_META
CONV_K = 4
HEAD = 128
EPS = 1e-6
Q_TILE = 256
K_TILE = 256
VMEM_LIMIT = 56 * 1024 * 1024
MASKED_LOGIT = -1e30
ATTN_UNROLL = 4

NT_DIMS = (((1,), (1,)), ((), ()))
TN_DIMS = (((0,), (0,)), ((), ()))


def _round_up(x, m):
    return -(-x // m) * m


def _unroll_for(n):
    for u in (5, 4, 3, 2):
        if n % u == 0:
            return u
    return 1


def _dot(a, b):
    return jnp.dot(a, b, preferred_element_type=F32)


def _rmsnorm_kernel(x_ref, w_ref, o_ref):
    x = x_ref[...]
    ms = jnp.mean(x * x, axis=-1, keepdims=True)
    o_ref[...] = (x * lax.rsqrt(ms + EPS) * w_ref[...]).astype(o_ref.dtype)


def _rmsnorm(x, w, tm):
    m, d = x.shape
    return pl.pallas_call(
        _rmsnorm_kernel,
        grid=(m // tm,),
        in_specs=[pl.BlockSpec((tm, d), lambda i: (i, 0)),
                  pl.BlockSpec((1, d), lambda i: (0, 0))],
        out_specs=pl.BlockSpec((tm, d), lambda i: (i, 0)),
        out_shape=jax.ShapeDtypeStruct((m, d), BF16),
        compiler_params=pltpu.CompilerParams(dimension_semantics=("parallel",),
                                             vmem_limit_bytes=VMEM_LIMIT),
        name="rmsnorm",
    )(x, w.reshape(1, d).astype(F32))


def _mm_kernel(*refs, nk, head_major, has_res):
    if has_res:
        a_ref, w_ref, r_ref, o_ref = refs[:4]
        acc_ref = refs[4] if nk > 1 else None
    else:
        a_ref, w_ref, o_ref = refs[:3]
        r_ref = None
        acc_ref = refs[3] if nk > 1 else None

    def finish(res):
        if has_res:
            res = res + r_ref[...]
        if head_major:
            for g in range(o_ref.shape[0]):
                o_ref[g] = res[:, g * LANES:(g + 1) * LANES].astype(o_ref.dtype)
        else:
            o_ref[...] = res.astype(o_ref.dtype)

    part = _dot(a_ref[...], w_ref[...])
    if nk == 1:
        finish(part)
    else:
        k = pl.program_id(2)

        @pl.when(k == 0)
        def _():
            acc_ref[...] = part

        @pl.when(jnp.logical_and(k > 0, k < nk - 1))
        def _():
            acc_ref[...] += part

        @pl.when(k == nk - 1)
        def _():
            finish(acc_ref[...] + part)


def _matmul(a, w, *, tm, tn, tk, head_major=False, res=None, out_dtype=F32, name="matmul"):
    m, kdim = a.shape
    n = w.shape[1]
    nk = kdim // tk
    in_specs = [pl.BlockSpec((tm, tk), lambda i, j, k: (i, k)),
                pl.BlockSpec((tk, tn), lambda i, j, k: (k, j))]
    args = [a, w]
    if res is not None:
        in_specs.append(pl.BlockSpec((tm, tn), lambda i, j, k: (i, j)))
        args.append(res)
    if head_major:
        out_spec = pl.BlockSpec((tn // LANES, tm, LANES), lambda i, j, k: (j, i, 0))
        out_shape = jax.ShapeDtypeStruct((n // LANES, m, LANES), out_dtype)
    else:
        out_spec = pl.BlockSpec((tm, tn), lambda i, j, k: (i, j))
        out_shape = jax.ShapeDtypeStruct((m, n), out_dtype)
    scratch = [pltpu.VMEM((tm, tn), F32)] if nk > 1 else []
    return pl.pallas_call(
        functools.partial(_mm_kernel, nk=nk, head_major=head_major, has_res=res is not None),
        grid=(m // tm, n // tn, nk),
        in_specs=in_specs,
        out_specs=out_spec,
        out_shape=out_shape,
        scratch_shapes=scratch,
        compiler_params=pltpu.CompilerParams(
            dimension_semantics=("parallel", "parallel", "arbitrary"),
            vmem_limit_bytes=VMEM_LIMIT),
        name=name,
    )(*args)


def _gate_kernel(ba_ref, alog_ref, dtb_ref, md_ref, mb_ref, rows_ref, *, npair):
    c = pl.program_id(1)
    half = LANES // 2
    lane = lax.broadcasted_iota(jnp.int32, (CHUNK, LANES), 1)
    sub = lax.broadcasted_iota(jnp.int32, (CHUNK, LANES), 0)
    is_b = lane < half
    jloc = jnp.where(is_b, lane, lane - half)
    low_incl = sub >= jloc
    low_strict = sub > jloc
    tri = (lax.broadcasted_iota(jnp.int32, (CHUNK, CHUNK), 0)
           >= lax.broadcasted_iota(jnp.int32, (CHUNK, CHUNK), 1)).astype(F32)
    neg_a = -jnp.exp(alog_ref[...])
    dtb = dtb_ref[...]

    def one_chunk(x, first):
        beta = jax.nn.sigmoid(x)
        g = neg_a * jax.nn.softplus(x + dtb)
        if first:
            real = sub >= PAD
            beta = jnp.where(real, beta, 0.0)
            g = jnp.where(real, g, 0.0)
        g = jnp.where(is_b, 0.0, g)
        gc = jnp.dot(tri, g, preferred_element_type=F32, precision=lax.Precision.HIGHEST)
        m1 = jnp.where(is_b, beta, gc)
        glb = jnp.broadcast_to(gc[CHUNK - 1:CHUNK, :], (CHUNK, LANES))
        zpad = jnp.zeros((LANES - CHUNK, LANES), F32)
        m1t = jnp.concatenate([m1, zpad], axis=0).T
        glt = jnp.concatenate([glb, zpad], axis=0).T
        q = npair
        beta_p = jnp.concatenate([m1t[0:q, :half], m1t[q:2 * q, :half]], axis=1)
        gc_p = jnp.concatenate([m1t[2 * q:3 * q, :half], m1t[3 * q:4 * q, :half]], axis=1)
        gl_p = jnp.concatenate([glt[2 * q:3 * q, :half], glt[3 * q:4 * q, :half]], axis=1)
        for p in range(npair):
            rows_ref[0, 0, p, pl.ds(c, 1), :] = beta_p[p:p + 1, :]
            rows_ref[0, 1, p, pl.ds(c, 1), :] = gc_p[p:p + 1, :]
            rows_ref[0, 2, p, pl.ds(c, 1), :] = gl_p[p:p + 1, :]
            gcol = jnp.where(is_b, gc[:, 2 * q + p:2 * q + p + 1], gc[:, 3 * q + p:3 * q + p + 1])
            bcol = jnp.where(is_b, beta[:, p:p + 1], beta[:, q + p:q + p + 1])
            dec = jnp.exp(gcol - gc_p[p:p + 1, :])
            md_ref[0, 0, p] = jnp.where(low_incl, dec, 0.0)
            mb_ref[0, 0, p] = jnp.where(low_strict, dec * bcol, 0.0)

    @pl.when(c == 0)
    def _():
        rows_ref[...] = jnp.zeros(rows_ref.shape, F32)
        one_chunk(jnp.concatenate([jnp.zeros((PAD, LANES), F32), ba_ref[0:N_META, :]], axis=0), True)

    @pl.when(c > 0)
    def _():
        r0 = pl.multiple_of(c * CHUNK - PAD, 16)
        one_chunk(ba_ref[pl.ds(r0, CHUNK), :], False)


def _gate_prep(ba, alog, dtb, *, bsz, lp, nc):
    npair = LANES // 4
    ncp = _round_up(nc, 8)
    big = jax.ShapeDtypeStruct((bsz, nc, npair, CHUNK, LANES), F32)
    return pl.pallas_call(
        functools.partial(_gate_kernel, npair=npair),
        grid=(bsz, nc),
        in_specs=[pl.BlockSpec((lp, LANES), lambda b, c: (b, 0)),
                  pl.BlockSpec((1, LANES), lambda b, c: (0, 0)),
                  pl.BlockSpec((1, LANES), lambda b, c: (0, 0))],
        out_specs=[pl.BlockSpec((1, 1, npair, CHUNK, LANES), lambda b, c: (b, c, 0, 0, 0)),
                   pl.BlockSpec((1, 1, npair, CHUNK, LANES), lambda b, c: (b, c, 0, 0, 0)),
                   pl.BlockSpec((1, 3, npair, ncp, LANES), lambda b, c: (b, 0, 0, 0, 0))],
        out_shape=[big, big, jax.ShapeDtypeStruct((bsz, 3, npair, ncp, LANES), F32)],
        compiler_params=pltpu.CompilerParams(dimension_semantics=("parallel", "arbitrary"),
                                             vmem_limit_bytes=VMEM_LIMIT),
        name="gate_prep",
    )(ba, alog, dtb)


def _delta_kernel(q_ref, k_ref, v_ref, z_ref, cwq_ref, cwk_ref, cwv_ref, onw_ref,
                  md_ref, mb_ref, rows_ref, o_ref,
                  qs, ks, vs, ob, wq_s, u_s, att_s, kd_s, st_s, *, lp, nc, rb):
    half = LANES // 2
    ls = qs.shape[0]
    nblk = lp // rb

    ob[0:16, 0:LANES] = jnp.zeros((16, LANES), F32)

    def conv_slab(load, w, store, normalize, scale):
        for blk in range(nblk):
            t0 = blk * rb
            ob[16 + t0:16 + t0 + rb, 0:LANES] = load(t0).astype(F32)
        for blk in range(nblk):
            t0 = blk * rb
            y = w[CONV_K - 1:CONV_K, :] * ob[16 + t0:16 + t0 + rb, 0:LANES]
            for j in range(CONV_K - 1):
                off = 16 + t0 - (CONV_K - 1) + j
                y = y + w[j:j + 1, :] * ob[off:off + rb, 0:LANES]
            y = y * jax.nn.sigmoid(y)
            if normalize:
                ss = jnp.sum(y * y, axis=-1, keepdims=True)
                y = y * (lax.rsqrt(ss + EPS) * scale)
            store(t0, y.astype(BF16))

    def store_to(ref):
        def f(t0, y):
            ref[PAD + t0:PAD + t0 + rb, :] = y
        return f

    def store_v(h):
        def f(t0, y):
            vs[h, PAD + t0:PAD + t0 + rb, :] = y
        return f

    zero_front = jnp.zeros((PAD, LANES), BF16)
    qs[0:PAD, :] = zero_front
    ks[0:PAD, :] = zero_front
    vs[0, 0:PAD, :] = zero_front
    vs[1, 0:PAD, :] = zero_front
    if ls > PAD + lp:
        tail = jnp.zeros((ls - PAD - lp, LANES), BF16)
        qs[PAD + lp:ls, :] = tail
        ks[PAD + lp:ls, :] = tail
        vs[0, PAD + lp:ls, :] = tail
        vs[1, PAD + lp:ls, :] = tail
    conv_slab(lambda t0: q_ref[0, t0:t0 + rb, :], cwq_ref[0], store_to(qs), True, HEAD ** -0.5)
    conv_slab(lambda t0: k_ref[0, t0:t0 + rb, :], cwk_ref[0], store_to(ks), True, 1.0)
    conv_slab(lambda t0: v_ref[0, t0:t0 + rb, :], cwv_ref[0], store_v(0), False, 1.0)
    conv_slab(lambda t0: v_ref[1, t0:t0 + rb, :], cwv_ref[1], store_v(1), False, 1.0)

    lane = lax.broadcasted_iota(jnp.int32, (CHUNK, LANES), 1)
    sub = lax.broadcasted_iota(jnp.int32, (CHUNK, LANES), 0)
    left = lane < half
    eye2 = (jnp.where(left, lane, lane - half) == sub).astype(F32)
    zero_b = jnp.zeros((CHUNK, LANES), BF16)
    zero_f = jnp.zeros((CHUNK, LANES), F32)

    def blockdiag(p):
        return jnp.concatenate([jnp.where(left, p, zero_f), jnp.where(left, zero_f, p)],
                               axis=0).astype(BF16)

    def blockdiag_wide(a, b):
        return jnp.concatenate([jnp.concatenate([a, zero_b], axis=1),
                                jnp.concatenate([zero_b, b], axis=1)], axis=0)

    grp = _unroll_for(nc)

    def local(i, carry):
        cs = [i * grp + g for g in range(grp)]
        r0s = [pl.multiple_of(c * CHUNK, CHUNK) for c in cs]
        kbs = [ks[pl.ds(r0, CHUNK), :] for r0 in r0s]
        qbs = [qs[pl.ds(r0, CHUNK), :] for r0 in r0s]
        a2s = [lax.dot_general(jnp.concatenate([kb, qb], axis=0), jnp.concatenate([kb, kb], axis=0),
                               NT_DIMS, preferred_element_type=F32) for kb, qb in zip(kbs, qbs)]
        ns = [-(a2[:CHUNK] * mb_ref[0, c, 0]) for a2, c in zip(a2s, cs)]
        atts = [a2[CHUNK:] * md_ref[0, c, 0] for a2, c in zip(a2s, cs)]
        ps = [_dot(n.astype(BF16), blockdiag(n)) for n in ns]
        ts = [eye2 + n for n in ns]
        for _ in range(4):
            rs = [_dot(jnp.concatenate([p, t], axis=0).astype(BF16), blockdiag(p)) for p, t in zip(ps, ts)]
            ps = [r[:CHUNK] for r in rs]
            ts = [t + r[CHUNK:] for t, r in zip(ts, rs)]
        ts = [t + _dot(t.astype(BF16), blockdiag(p)) for t, p in zip(ts, ps)]

        beta_ps = [rows_ref[0, 0, 0, pl.ds(c, 1), :] for c in cs]
        gc_ps = [rows_ref[0, 1, 0, pl.ds(c, 1), :] for c in cs]
        gl_ps = [rows_ref[0, 2, 0, pl.ds(c, 1), :] for c in cs]
        egcs = [jnp.exp(gc_p) for gc_p in gc_ps]
        ekds = [jnp.exp(gl_p - gc_p) for gl_p, gc_p in zip(gl_ps, gc_ps)]
        u2s = [_dot((t * beta_p).astype(BF16),
                    blockdiag_wide(vs[0, pl.ds(r0, CHUNK), :], vs[1, pl.ds(r0, CHUNK), :]))
               for t, beta_p, r0 in zip(ts, beta_ps, r0s)]
        wks = [_dot(jnp.concatenate([(t * (beta_p * egc)).astype(BF16), (eye2 * ekd).astype(BF16)], axis=0),
                    blockdiag_wide(kb, kb))
               for t, beta_p, egc, ekd, kb in zip(ts, beta_ps, egcs, ekds, kbs)]
        qd2s = [_dot((eye2 * egc).astype(BF16), blockdiag_wide(qb, qb)) for egc, qb in zip(egcs, qbs)]
        for g, c in enumerate(cs):
            for h in range(2):
                sl = slice(h * LANES, (h + 1) * LANES)
                wq_s[c, h] = jnp.concatenate([wks[g][:CHUNK, sl], qd2s[g][:, sl]], axis=0).astype(BF16)
            u_s[c] = u2s[g]
            att_s[c] = atts[g].astype(BF16)
            kd_s[c] = wks[g][CHUNK:].astype(BF16)
        return carry

    lax.fori_loop(0, nc // grp, local, 0)

    st_s[...] = jnp.zeros(st_s.shape, F32)
    if ls > nc * CHUNK:
        ob[nc * CHUNK:ls, :] = jnp.zeros((ls - nc * CHUNK, 2 * LANES), F32)
    left8 = lax.broadcasted_iota(jnp.int32, (8, LANES), 1) < half

    def state(c, carry):
        r0 = pl.multiple_of(c * CHUNK, CHUNK)
        egl = jnp.broadcast_to(jnp.exp(rows_ref[0, 2, 0, pl.ds(c, 1), :]), (8, LANES))
        egl_sw = pltpu.roll(egl, half, 1)
        egl_h = (jnp.where(left8, egl, egl_sw)[0:1], jnp.where(left8, egl_sw, egl)[0:1])
        u2 = u_s[c]
        rs, vn = [], []
        for h in range(2):
            r = _dot(wq_s[c, h], st_s[h].astype(BF16))
            rs.append(r)
            vn.append((u2[:, h * LANES:(h + 1) * LANES] - r[:CHUNK]).astype(BF16))
        o2 = _dot(att_s[c], blockdiag_wide(vn[0], vn[1]))
        ob[pl.ds(r0, CHUNK), :] = o2 + jnp.concatenate([rs[0][CHUNK:], rs[1][CHUNK:]], axis=1)
        kd2 = kd_s[c]
        for h in range(2):
            upd = lax.dot_general(kd2[:, h * LANES:(h + 1) * LANES], vn[h], TN_DIMS,
                                  preferred_element_type=F32)
            st_s[h] = st_s[h] * egl_h[h] + upd
        return carry

    lax.fori_loop(0, nc, state, 0)

    onw = onw_ref[...]
    for blk in range(nblk):
        t0 = blk * rb
        for h in range(2):
            o = ob[PAD + t0:PAD + t0 + rb, h * LANES:(h + 1) * LANES]
            ms = jnp.mean(o * o, axis=-1, keepdims=True)
            zz = z_ref[h, t0:t0 + rb, :].astype(F32)
            o_ref[t0:t0 + rb, h * LANES:(h + 1) * LANES] = (
                o * lax.rsqrt(ms + EPS) * onw * (zz * jax.nn.sigmoid(zz))).astype(o_ref.dtype)


def _delta_rule(pj, cw, onw, md, mb, rows, *, bsz, lp, nc, hk_n):
    ls = max(PAD + lp, nc * CHUNK)
    rb = lp // 4 if lp % 32 == 0 and lp >= 1024 else lp
    kern = functools.partial(_delta_kernel, lp=lp, nc=nc, rb=rb)
    return pl.pallas_call(
        kern,
        grid=(bsz, hk_n),
        in_specs=[
            pl.BlockSpec((1, lp, LANES), lambda b, h: (h, b, 0)),
            pl.BlockSpec((1, lp, LANES), lambda b, h: (hk_n + h, b, 0)),
            pl.BlockSpec((2, lp, LANES), lambda b, h: (hk_n + h, b, 0)),
            pl.BlockSpec((2, lp, LANES), lambda b, h: (2 * hk_n + h, b, 0)),
            pl.BlockSpec((1, CONV_K, LANES), lambda b, h: (h, 0, 0)),
            pl.BlockSpec((1, CONV_K, LANES), lambda b, h: (hk_n + h, 0, 0)),
            pl.BlockSpec((2, CONV_K, LANES), lambda b, h: (hk_n + h, 0, 0)),
            pl.BlockSpec((1, LANES), lambda b, h: (0, 0)),
            pl.BlockSpec((1, nc, 1, CHUNK, LANES), lambda b, h: (b, 0, h, 0, 0)),
            pl.BlockSpec((1, nc, 1, CHUNK, LANES), lambda b, h: (b, 0, h, 0, 0)),
            pl.BlockSpec((1, 3, 1, _round_up(nc, 8), LANES), lambda b, h: (b, 0, h, 0, 0)),
        ],
        out_specs=pl.BlockSpec((lp, 2 * LANES), lambda b, h: (b, h)),
        out_shape=jax.ShapeDtypeStruct((bsz * lp, 2 * hk_n * LANES), BF16),
        scratch_shapes=[
            pltpu.VMEM((ls, LANES), BF16),
            pltpu.VMEM((ls, LANES), BF16),
            pltpu.VMEM((2, ls, LANES), BF16),
            pltpu.VMEM((ls, 2 * LANES), F32),
            pltpu.VMEM((nc, 2, 2 * CHUNK, LANES), BF16),
            pltpu.VMEM((nc, CHUNK, 2 * LANES), F32),
            pltpu.VMEM((nc, CHUNK, LANES), BF16),
            pltpu.VMEM((nc, CHUNK, 2 * LANES), BF16),
            pltpu.VMEM((2, HEAD, HEAD), F32),
        ],
        compiler_params=pltpu.CompilerParams(dimension_semantics=("parallel", "arbitrary"),
                                             vmem_limit_bytes=VMEM_LIMIT),
        name="delta_rule",
    )(pj, pj, pj, pj, cw, cw, cw, onw, md, mb, rows)


def _attn_kernel(q_ref, k_ref, v_ref, g_ref, qw_ref, kw_ref, o_ref, qn, kn, vn, acc_s, *, lp, rb):
    lk = kn.shape[0]
    nblk = lp // rb
    scale = HEAD ** -0.5
    if lk > lp:
        zpad = jnp.zeros((lk - lp, HEAD), BF16)
        qn[lp:lk, :] = zpad
        kn[lp:lk, :] = zpad
        vn[lp:lk, :] = zpad
    qw = qw_ref[...]
    kw = kw_ref[...]
    for blk in range(nblk):
        t0 = blk * rb
        x = q_ref[0, t0:t0 + rb, :].astype(F32)
        ms = jnp.mean(x * x, axis=-1, keepdims=True)
        qn[t0:t0 + rb, :] = (x * lax.rsqrt(ms + EPS) * (qw * scale)).astype(BF16)
        x = k_ref[0, t0:t0 + rb, :].astype(F32)
        ms = jnp.mean(x * x, axis=-1, keepdims=True)
        kn[t0:t0 + rb, :] = (x * lax.rsqrt(ms + EPS) * kw).astype(BF16)
        vn[t0:t0 + rb, :] = v_ref[0, t0:t0 + rb, :]

    neg_upper = -(lax.broadcasted_iota(jnp.int32, (K_TILE, K_TILE), 0)
                  > lax.broadcasted_iota(jnp.int32, (K_TILE, K_TILE), 1)).astype(BF16)
    col_minus_row = (lax.broadcasted_iota(jnp.int32, (Q_TILE, K_TILE), 1)
                     - lax.broadcasted_iota(jnp.int32, (Q_TILE, K_TILE), 0))

    def q_loop(qi, carry):
        q0 = pl.multiple_of(qi * Q_TILE, Q_TILE)
        q = qn[pl.ds(q0, Q_TILE), :]

        def k_loop(j, c2):
            acc, cr = c2
            kts = [qi - j * ATTN_UNROLL - u for u in range(ATTN_UNROLL)]
            k0s = [pl.multiple_of(jnp.maximum(kt, 0) * K_TILE, K_TILE) for kt in kts]
            zs_l, sp_l, tl_l = [], [], []
            z_l = [lax.dot_general(q, kn[pl.ds(k0, K_TILE), :], NT_DIMS, preferred_element_type=F32)
                   for k0 in k0s]
            for u, z in enumerate(z_l):
                sp = jnp.maximum(z, 0.0) + jnp.log(1.0 + jnp.exp(-jnp.abs(z)))
                zs = z - sp
                if u == 0:
                    causal = col_minus_row < (qi - kts[u]) * Q_TILE
                    zs = jnp.where(causal, zs, MASKED_LOGIT)
                    sp = jnp.where(causal, sp, 0.0)
                zs_l.append(zs)
                sp_l.append(sp)
            for sp in sp_l:
                tl_l.append(_dot(sp.astype(BF16), neg_upper))
            a_l = []
            for u in range(ATTN_UNROLL):
                a_l.append(jnp.exp(zs_l[u] + tl_l[u] + (cr + jnp.where(kts[u] >= 0, 0.0, MASKED_LOGIT))))
                cr = cr + tl_l[u][:, 0:1] - sp_l[u][:, 0:1]
            for u in range(ATTN_UNROLL):
                acc = acc + _dot(a_l[u].astype(BF16), vn[pl.ds(k0s[u], K_TILE), :])
            return acc, cr

        acc, _ = lax.fori_loop(0, qi // ATTN_UNROLL + 1, k_loop,
                               (jnp.zeros((Q_TILE, HEAD), F32), jnp.zeros((Q_TILE, 1), F32)))
        acc_s[pl.ds(q0, Q_TILE), :] = acc
        return carry

    lax.fori_loop(0, lk // Q_TILE, q_loop, 0)

    for blk in range(nblk):
        t0 = blk * rb
        g = g_ref[0, t0:t0 + rb, :].astype(F32)
        o_ref[t0:t0 + rb, :] = (acc_s[t0:t0 + rb, :] * (g * jax.nn.sigmoid(g))).astype(o_ref.dtype)


def _sb_attention(pj, qw, kw, *, bsz, lp, nh):
    lk = _round_up(lp, K_TILE)
    rb = lp // 4 if lp % 64 == 0 and lp >= 1024 else lp
    return pl.pallas_call(
        functools.partial(_attn_kernel, lp=lp, rb=rb),
        grid=(bsz, nh),
        in_specs=[
            pl.BlockSpec((1, lp, HEAD), lambda b, h: (h, b, 0)),
            pl.BlockSpec((1, lp, HEAD), lambda b, h: (nh + h, b, 0)),
            pl.BlockSpec((1, lp, HEAD), lambda b, h: (2 * nh + h, b, 0)),
            pl.BlockSpec((1, lp, HEAD), lambda b, h: (3 * nh + h, b, 0)),
            pl.BlockSpec((1, HEAD), lambda b, h: (0, 0)),
            pl.BlockSpec((1, HEAD), lambda b, h: (0, 0)),
        ],
        out_specs=pl.BlockSpec((lp, HEAD), lambda b, h: (b, h)),
        out_shape=jax.ShapeDtypeStruct((bsz * lp, nh * HEAD), BF16),
        scratch_shapes=[pltpu.VMEM((lk, HEAD), BF16),
                        pltpu.VMEM((lk, HEAD), BF16),
                        pltpu.VMEM((lk, HEAD), BF16),
                        pltpu.VMEM((lk, HEAD), F32)],
        compiler_params=pltpu.CompilerParams(dimension_semantics=("parallel", "parallel"),
                                             vmem_limit_bytes=VMEM_LIMIT),
        name="sb_attention",
    )(pj, pj, pj, pj, qw, kw)


def _pick_tile(total, prefs):
    for t in prefs:
        if total % t == 0:
            return t
    return total


def kernel(x, meta_tokens, dn_norm_w, dn_w_in, dn_conv_w, dn_a_log, dn_dt_bias, dn_out_norm_w, dn_w_out,
           sb_norm_w, sb_w_in, sb_q_norm_w, sb_k_norm_w, sb_w_out):
    bsz, seq, d = x.shape
    hk_n = d // HEAD
    hv_n = 2 * hk_n
    assert hv_n == LANES // 2, "gate layout assumes 64 value heads"
    l_real = N_META + seq
    lp = _round_up(l_real, LANES)
    nc = -(-(PAD + l_real) // CHUNK)
    m = bsz * lp
    tm = _pick_tile(m, (768, 512, 384, 256, 128))
    tn_rms = _pick_tile(m, (256, 128))

    meta = jnp.broadcast_to(meta_tokens[None].astype(x.dtype), (bsz, N_META, d))
    h0 = jnp.concatenate([meta, x, jnp.zeros((bsz, lp - l_real, d), x.dtype)], axis=1).reshape(m, d)

    n_main = 2 * hk_n * HEAD + 2 * hv_n * HEAD
    w_main = dn_w_in[0, :, :n_main].astype(BF16)
    perm = jnp.concatenate([jnp.arange(0, hv_n, 2), jnp.arange(1, hv_n, 2)])
    perm = jnp.concatenate([perm, hv_n + perm])
    w_ba = dn_w_in[0, :, n_main:][:, perm].astype(BF16)
    half_perm = perm[:hv_n]
    zeros64 = jnp.zeros((hv_n,), F32)
    alog = jnp.concatenate([zeros64, dn_a_log[0][half_perm].astype(F32)]).reshape(1, LANES)
    dtb = jnp.concatenate([zeros64, dn_dt_bias[0][half_perm].astype(F32)]).reshape(1, LANES)

    hn = _rmsnorm(h0, dn_norm_w[0], tn_rms)
    pj0 = _matmul(hn, w_main, tm=tm, tn=1024, tk=d, head_major=True, out_dtype=BF16, name="dn_in_proj")
    ba = _matmul(hn, w_ba, tm=tm, tn=LANES, tk=d, name="dn_gate_proj")
    md, mb, rows = _gate_prep(ba, alog, dtb, bsz=bsz, lp=lp, nc=nc)
    cw = dn_conv_w[0].astype(F32).reshape(CONV_K, -1, LANES).transpose(1, 0, 2)
    o0 = _delta_rule(pj0, cw, dn_out_norm_w[0].astype(F32).reshape(1, LANES), md, mb, rows,
                     bsz=bsz, lp=lp, nc=nc, hk_n=hk_n)
    h1 = _matmul(o0, dn_w_out[0].astype(BF16), tm=tm, tn=512, tk=2 * d, res=h0, name="dn_out_proj")

    hn1 = _rmsnorm(h1, sb_norm_w[0], tn_rms)
    pj1 = _matmul(hn1, sb_w_in[0].astype(BF16), tm=tm, tn=1024, tk=d, head_major=True, out_dtype=BF16,
                  name="sb_in_proj")
    o1 = _sb_attention(pj1, sb_q_norm_w[0].astype(F32).reshape(1, HEAD),
                       sb_k_norm_w[0].astype(F32).reshape(1, HEAD), bsz=bsz, lp=lp, nh=hk_n)
    h2 = _matmul(o1, sb_w_out[0].astype(BF16), tm=tm, tn=1024, tk=d, res=h1, name="sb_out_proj")
    return h2.reshape(bsz, lp, d)[:, N_META:l_real]
```

```python
import functools

import jax
import jax.numpy as jnp
from jax import lax
from jax.experimental import pallas as pl
from jax.experimental.pallas import tpu as pltpu

F32 = jnp.float32
BF16 = jnp.bfloat16

LANES = 128
CHUNK = 64
N_META = 16
PAD = CHUNK - N_META
CONV_K = 4
HEAD = 128
EPS = 1e-6
Q_TILE = 256
K_TILE = 256
VMEM_LIMIT = 56 * 1024 * 1024
MASKED_LOGIT = -1e30
ATTN_UNROLL = 4

NT_DIMS = (((1,), (1,)), ((), ()))
TN_DIMS = (((0,), (0,)), ((), ()))


def _round_up(x, m):
    return -(-x // m) * m


def _unroll_for(n):
    for u in (5, 4, 3, 2):
        if n % u == 0:
            return u
    return 1


def _dot(a, b):
    return jnp.dot(a, b, preferred_element_type=F32)


def _rmsnorm_kernel(x_ref, w_ref, o_ref):
    x = x_ref[...]
    ms = jnp.mean(x * x, axis=-1, keepdims=True)
    o_ref[...] = (x * lax.rsqrt(ms + EPS) * w_ref[...]).astype(o_ref.dtype)


def _rmsnorm(x, w, tm):
    m, d = x.shape
    return pl.pallas_call(
        _rmsnorm_kernel,
        grid=(m // tm,),
        in_specs=[pl.BlockSpec((tm, d), lambda i: (i, 0)),
                  pl.BlockSpec((1, d), lambda i: (0, 0))],
        out_specs=pl.BlockSpec((tm, d), lambda i: (i, 0)),
        out_shape=jax.ShapeDtypeStruct((m, d), BF16),
        compiler_params=pltpu.CompilerParams(dimension_semantics=("parallel",),
                                             vmem_limit_bytes=VMEM_LIMIT),
        name="rmsnorm",
    )(x, w.reshape(1, d).astype(F32))


def _mm_kernel(*refs, nk, head_major, has_res):
    if has_res:
        a_ref, w_ref, r_ref, o_ref = refs[:4]
        acc_ref = refs[4] if nk > 1 else None
    else:
        a_ref, w_ref, o_ref = refs[:3]
        r_ref = None
        acc_ref = refs[3] if nk > 1 else None

    def finish(res):
        if has_res:
            res = res + r_ref[...]
        if head_major:
            for g in range(o_ref.shape[0]):
                o_ref[g] = res[:, g * LANES:(g + 1) * LANES].astype(o_ref.dtype)
        else:
            o_ref[...] = res.astype(o_ref.dtype)

    part = _dot(a_ref[...], w_ref[...])
    if nk == 1:
        finish(part)
    else:
        k = pl.program_id(2)

        @pl.when(k == 0)
        def _():
            acc_ref[...] = part

        @pl.when(jnp.logical_and(k > 0, k < nk - 1))
        def _():
            acc_ref[...] += part

        @pl.when(k == nk - 1)
        def _():
            finish(acc_ref[...] + part)


def _matmul(a, w, *, tm, tn, tk, head_major=False, res=None, out_dtype=F32, name="matmul"):
    m, kdim = a.shape
    n = w.shape[1]
    nk = kdim // tk
    in_specs = [pl.BlockSpec((tm, tk), lambda i, j, k: (i, k)),
                pl.BlockSpec((tk, tn), lambda i, j, k: (k, j))]
    args = [a, w]
    if res is not None:
        in_specs.append(pl.BlockSpec((tm, tn), lambda i, j, k: (i, j)))
        args.append(res)
    if head_major:
        out_spec = pl.BlockSpec((tn // LANES, tm, LANES), lambda i, j, k: (j, i, 0))
        out_shape = jax.ShapeDtypeStruct((n // LANES, m, LANES), out_dtype)
    else:
        out_spec = pl.BlockSpec((tm, tn), lambda i, j, k: (i, j))
        out_shape = jax.ShapeDtypeStruct((m, n), out_dtype)
    scratch = [pltpu.VMEM((tm, tn), F32)] if nk > 1 else []
    return pl.pallas_call(
        functools.partial(_mm_kernel, nk=nk, head_major=head_major, has_res=res is not None),
        grid=(m // tm, n // tn, nk),
        in_specs=in_specs,
        out_specs=out_spec,
        out_shape=out_shape,
        scratch_shapes=scratch,
        compiler_params=pltpu.CompilerParams(
            dimension_semantics=("parallel", "parallel", "arbitrary"),
            vmem_limit_bytes=VMEM_LIMIT),
        name=name,
    )(*args)


def _gate_kernel(ba_ref, alog_ref, dtb_ref, md_ref, mb_ref, rows_ref, *, npair):
    c = pl.program_id(1)
    half = LANES // 2
    lane = lax.broadcasted_iota(jnp.int32, (CHUNK, LANES), 1)
    sub = lax.broadcasted_iota(jnp.int32, (CHUNK, LANES), 0)
    is_b = lane < half
    jloc = jnp.where(is_b, lane, lane - half)
    low_incl = sub >= jloc
    low_strict = sub > jloc
    tri = (lax.broadcasted_iota(jnp.int32, (CHUNK, CHUNK), 0)
           >= lax.broadcasted_iota(jnp.int32, (CHUNK, CHUNK), 1)).astype(F32)
    neg_a = -jnp.exp(alog_ref[...])
    dtb = dtb_ref[...]

    def one_chunk(x, first):
        beta = jax.nn.sigmoid(x)
        g = neg_a * jax.nn.softplus(x + dtb)
        if first:
            real = sub >= PAD
            beta = jnp.where(real, beta, 0.0)
            g = jnp.where(real, g, 0.0)
        g = jnp.where(is_b, 0.0, g)
        gc = jnp.dot(tri, g, preferred_element_type=F32, precision=lax.Precision.HIGHEST)
        m1 = jnp.where(is_b, beta, gc)
        glb = jnp.broadcast_to(gc[CHUNK - 1:CHUNK, :], (CHUNK, LANES))
        zpad = jnp.zeros((LANES - CHUNK, LANES), F32)
        m1t = jnp.concatenate([m1, zpad], axis=0).T
        glt = jnp.concatenate([glb, zpad], axis=0).T
        q = npair
        beta_p = jnp.concatenate([m1t[0:q, :half], m1t[q:2 * q, :half]], axis=1)
        gc_p = jnp.concatenate([m1t[2 * q:3 * q, :half], m1t[3 * q:4 * q, :half]], axis=1)
        gl_p = jnp.concatenate([glt[2 * q:3 * q, :half], glt[3 * q:4 * q, :half]], axis=1)
        for p in range(npair):
            rows_ref[0, 0, p, pl.ds(c, 1), :] = beta_p[p:p + 1, :]
            rows_ref[0, 1, p, pl.ds(c, 1), :] = gc_p[p:p + 1, :]
            rows_ref[0, 2, p, pl.ds(c, 1), :] = gl_p[p:p + 1, :]
            gcol = jnp.where(is_b, gc[:, 2 * q + p:2 * q + p + 1], gc[:, 3 * q + p:3 * q + p + 1])
            bcol = jnp.where(is_b, beta[:, p:p + 1], beta[:, q + p:q + p + 1])
            dec = jnp.exp(gcol - gc_p[p:p + 1, :])
            md_ref[0, 0, p] = jnp.where(low_incl, dec, 0.0)
            mb_ref[0, 0, p] = jnp.where(low_strict, dec * bcol, 0.0)

    @pl.when(c == 0)
    def _():
        rows_ref[...] = jnp.zeros(rows_ref.shape, F32)
        one_chunk(jnp.concatenate([jnp.zeros((PAD, LANES), F32), ba_ref[0:N_META, :]], axis=0), True)

    @pl.when(c > 0)
    def _():
        r0 = pl.multiple_of(c * CHUNK - PAD, 16)
        one_chunk(ba_ref[pl.ds(r0, CHUNK), :], False)


def _gate_prep(ba, alog, dtb, *, bsz, lp, nc):
    npair = LANES // 4
    ncp = _round_up(nc, 8)
    big = jax.ShapeDtypeStruct((bsz, nc, npair, CHUNK, LANES), F32)
    return pl.pallas_call(
        functools.partial(_gate_kernel, npair=npair),
        grid=(bsz, nc),
        in_specs=[pl.BlockSpec((lp, LANES), lambda b, c: (b, 0)),
                  pl.BlockSpec((1, LANES), lambda b, c: (0, 0)),
                  pl.BlockSpec((1, LANES), lambda b, c: (0, 0))],
        out_specs=[pl.BlockSpec((1, 1, npair, CHUNK, LANES), lambda b, c: (b, c, 0, 0, 0)),
                   pl.BlockSpec((1, 1, npair, CHUNK, LANES), lambda b, c: (b, c, 0, 0, 0)),
                   pl.BlockSpec((1, 3, npair, ncp, LANES), lambda b, c: (b, 0, 0, 0, 0))],
        out_shape=[big, big, jax.ShapeDtypeStruct((bsz, 3, npair, ncp, LANES), F32)],
        compiler_params=pltpu.CompilerParams(dimension_semantics=("parallel", "arbitrary"),
                                             vmem_limit_bytes=VMEM_LIMIT),
        name="gate_prep",
    )(ba, alog, dtb)


def _delta_kernel(q_ref, k_ref, v_ref, z_ref, cwq_ref, cwk_ref, cwv_ref, onw_ref,
                  md_ref, mb_ref, rows_ref, o_ref,
                  qs, ks, vs, ob, wq_s, u_s, att_s, kd_s, st_s, *, lp, nc, rb):
    half = LANES // 2
    ls = qs.shape[0]
    nblk = lp // rb

    ob[0:16, 0:LANES] = jnp.zeros((16, LANES), F32)

    def conv_slab(load, w, store, normalize, scale):
        for blk in range(nblk):
            t0 = blk * rb
            ob[16 + t0:16 + t0 + rb, 0:LANES] = load(t0).astype(F32)
        for blk in range(nblk):
            t0 = blk * rb
            y = w[CONV_K - 1:CONV_K, :] * ob[16 + t0:16 + t0 + rb, 0:LANES]
            for j in range(CONV_K - 1):
                off = 16 + t0 - (CONV_K - 1) + j
                y = y + w[j:j + 1, :] * ob[off:off + rb, 0:LANES]
            y = y * jax.nn.sigmoid(y)
            if normalize:
                ss = jnp.sum(y * y, axis=-1, keepdims=True)
                y = y * (lax.rsqrt(ss + EPS) * scale)
            store(t0, y.astype(BF16))

    def store_to(ref):
        def f(t0, y):
            ref[PAD + t0:PAD + t0 + rb, :] = y
        return f

    def store_v(h):
        def f(t0, y):
            vs[h, PAD + t0:PAD + t0 + rb, :] = y
        return f

    zero_front = jnp.zeros((PAD, LANES), BF16)
    qs[0:PAD, :] = zero_front
    ks[0:PAD, :] = zero_front
    vs[0, 0:PAD, :] = zero_front
    vs[1, 0:PAD, :] = zero_front
    if ls > PAD + lp:
        tail = jnp.zeros((ls - PAD - lp, LANES), BF16)
        qs[PAD + lp:ls, :] = tail
        ks[PAD + lp:ls, :] = tail
        vs[0, PAD + lp:ls, :] = tail
        vs[1, PAD + lp:ls, :] = tail
    conv_slab(lambda t0: q_ref[0, t0:t0 + rb, :], cwq_ref[0], store_to(qs), True, HEAD ** -0.5)
    conv_slab(lambda t0: k_ref[0, t0:t0 + rb, :], cwk_ref[0], store_to(ks), True, 1.0)
    conv_slab(lambda t0: v_ref[0, t0:t0 + rb, :], cwv_ref[0], store_v(0), False, 1.0)
    conv_slab(lambda t0: v_ref[1, t0:t0 + rb, :], cwv_ref[1], store_v(1), False, 1.0)

    lane = lax.broadcasted_iota(jnp.int32, (CHUNK, LANES), 1)
    sub = lax.broadcasted_iota(jnp.int32, (CHUNK, LANES), 0)
    left = lane < half
    eye2 = (jnp.where(left, lane, lane - half) == sub).astype(F32)
    zero_b = jnp.zeros((CHUNK, LANES), BF16)
    zero_f = jnp.zeros((CHUNK, LANES), F32)

    def blockdiag(p):
        return jnp.concatenate([jnp.where(left, p, zero_f), jnp.where(left, zero_f, p)],
                               axis=0).astype(BF16)

    def blockdiag_wide(a, b):
        return jnp.concatenate([jnp.concatenate([a, zero_b], axis=1),
                                jnp.concatenate([zero_b, b], axis=1)], axis=0)

    grp = _unroll_for(nc)
    ngrp = nc // grp
    st_s[...] = jnp.zeros(st_s.shape, F32)
    if ls > nc * CHUNK:
        ob[nc * CHUNK:ls, :] = jnp.zeros((ls - nc * CHUNK, 2 * LANES), F32)
    left8 = lax.broadcasted_iota(jnp.int32, (8, LANES), 1) < half

    def local_steps(i):
        cs = [i * grp + g for g in range(grp)]
        r0s = [pl.multiple_of(c * CHUNK, CHUNK) for c in cs]
        v = {}

        def gram():
            v["kb"] = [ks[pl.ds(r0, CHUNK), :] for r0 in r0s]
            v["qb"] = [qs[pl.ds(r0, CHUNK), :] for r0 in r0s]
            a2s = [lax.dot_general(jnp.concatenate([kb, qb], axis=0), jnp.concatenate([kb, kb], axis=0),
                                   NT_DIMS, preferred_element_type=F32)
                   for kb, qb in zip(v["kb"], v["qb"])]
            v["n"] = [-(a2[:CHUNK] * mb_ref[0, c, 0]) for a2, c in zip(a2s, cs)]
            v["att"] = [a2[CHUNK:] * md_ref[0, c, 0] for a2, c in zip(a2s, cs)]

        def square_first():
            v["p"] = [_dot(n.astype(BF16), blockdiag(n)) for n in v["n"]]
            v["t"] = [eye2 + n for n in v["n"]]

        def square_next():
            rs = [_dot(jnp.concatenate([p, t], axis=0).astype(BF16), blockdiag(p))
                  for p, t in zip(v["p"], v["t"])]
            v["p"] = [r[:CHUNK] for r in rs]
            v["t"] = [t + r[CHUNK:] for t, r in zip(v["t"], rs)]

        def square_last():
            v["t"] = [t + _dot(t.astype(BF16), blockdiag(p)) for t, p in zip(v["t"], v["p"])]
            v["beta"] = [rows_ref[0, 0, 0, pl.ds(c, 1), :] for c in cs]
            gc_ps = [rows_ref[0, 1, 0, pl.ds(c, 1), :] for c in cs]
            gl_ps = [rows_ref[0, 2, 0, pl.ds(c, 1), :] for c in cs]
            v["egc"] = [jnp.exp(gc_p) for gc_p in gc_ps]
            v["ekd"] = [jnp.exp(gl_p - gc_p) for gl_p, gc_p in zip(gl_ps, gc_ps)]

        def make_u():
            u2s = [_dot((t * beta_p).astype(BF16),
                        blockdiag_wide(vs[0, pl.ds(r0, CHUNK), :], vs[1, pl.ds(r0, CHUNK), :]))
                   for t, beta_p, r0 in zip(v["t"], v["beta"], r0s)]
            for c, u2 in zip(cs, u2s):
                u_s[c] = u2

        def make_w():
            v["wk"] = [_dot(jnp.concatenate([(t * (beta_p * egc)).astype(BF16),
                                             (eye2 * ekd).astype(BF16)], axis=0),
                            blockdiag_wide(kb, kb))
                       for t, beta_p, egc, ekd, kb in zip(v["t"], v["beta"], v["egc"], v["ekd"], v["kb"])]

        def make_qd():
            qd2s = [_dot((eye2 * egc).astype(BF16), blockdiag_wide(qb, qb))
                    for egc, qb in zip(v["egc"], v["qb"])]
            for g, c in enumerate(cs):
                for h in range(2):
                    sl = slice(h * LANES, (h + 1) * LANES)
                    wq_s[c, h] = jnp.concatenate([v["wk"][g][:CHUNK, sl], qd2s[g][:, sl]],
                                                 axis=0).astype(BF16)
                att_s[c] = v["att"][g].astype(BF16)
                kd_s[c] = v["wk"][g][CHUNK:].astype(BF16)

        return [gram, square_first, square_next, square_next, square_next, square_next, square_last,
                make_u, make_w, make_qd]

    def state_steps(i):
        steps = []
        for g in range(grp):
            c = i * grp + g
            v = {}

            def read_state(c=c, v=v):
                u2 = u_s[c]
                v["r"] = [_dot(wq_s[c, h], st_s[h].astype(BF16)) for h in range(2)]
                v["vn"] = [(u2[:, h * LANES:(h + 1) * LANES] - v["r"][h][:CHUNK]).astype(BF16)
                           for h in range(2)]

            def write_state(c=c, v=v):
                r0 = pl.multiple_of(c * CHUNK, CHUNK)
                egl = jnp.broadcast_to(jnp.exp(rows_ref[0, 2, 0, pl.ds(c, 1), :]), (8, LANES))
                egl_sw = pltpu.roll(egl, half, 1)
                egl_h = (jnp.where(left8, egl, egl_sw)[0:1], jnp.where(left8, egl_sw, egl)[0:1])
                o2 = _dot(att_s[c], blockdiag_wide(v["vn"][0], v["vn"][1]))
                ob[pl.ds(r0, CHUNK), :] = o2 + jnp.concatenate([v["r"][0][CHUNK:], v["r"][1][CHUNK:]],
                                                              axis=1)
                kd2 = kd_s[c]
                for h in range(2):
                    upd = lax.dot_general(kd2[:, h * LANES:(h + 1) * LANES], v["vn"][h], TN_DIMS,
                                          preferred_element_type=F32)
                    st_s[h] = st_s[h] * egl_h[h] + upd

            steps += [read_state, write_state]
        return steps

    def run_interleaved(first, second):
        n1, n2 = len(first), len(second)
        j = 0
        for k, step in enumerate(first):
            step()
            while j < n2 and (j + 1) * n1 <= (k + 1) * n2:
                second[j]()
                j += 1

    run_interleaved(local_steps(jnp.int32(0)), [])

    def body(i, carry):
        run_interleaved(state_steps(i), local_steps(i + 1))
        return carry

    lax.fori_loop(0, ngrp - 1, body, 0)
    run_interleaved(state_steps(jnp.int32(ngrp - 1)), [])

    onw = onw_ref[...]
    for blk in range(nblk):
        t0 = blk * rb
        for h in range(2):
            o = ob[PAD + t0:PAD + t0 + rb, h * LANES:(h + 1) * LANES]
            ms = jnp.mean(o * o, axis=-1, keepdims=True)
            zz = z_ref[h, t0:t0 + rb, :].astype(F32)
            o_ref[t0:t0 + rb, h * LANES:(h + 1) * LANES] = (
                o * lax.rsqrt(ms + EPS) * onw * (zz * jax.nn.sigmoid(zz))).astype(o_ref.dtype)


def _delta_rule(pj, cw, onw, md, mb, rows, *, bsz, lp, nc, hk_n):
    ls = max(PAD + lp, nc * CHUNK)
    rb = lp // 4 if lp % 32 == 0 and lp >= 1024 else lp
    kern = functools.partial(_delta_kernel, lp=lp, nc=nc, rb=rb)
    return pl.pallas_call(
        kern,
        grid=(bsz, hk_n),
        in_specs=[
            pl.BlockSpec((1, lp, LANES), lambda b, h: (h, b, 0)),
            pl.BlockSpec((1, lp, LANES), lambda b, h: (hk_n + h, b, 0)),
            pl.BlockSpec((2, lp, LANES), lambda b, h: (hk_n + h, b, 0)),
            pl.BlockSpec((2, lp, LANES), lambda b, h: (2 * hk_n + h, b, 0)),
            pl.BlockSpec((1, CONV_K, LANES), lambda b, h: (h, 0, 0)),
            pl.BlockSpec((1, CONV_K, LANES), lambda b, h: (hk_n + h, 0, 0)),
            pl.BlockSpec((2, CONV_K, LANES), lambda b, h: (hk_n + h, 0, 0)),
            pl.BlockSpec((1, LANES), lambda b, h: (0, 0)),
            pl.BlockSpec((1, nc, 1, CHUNK, LANES), lambda b, h: (b, 0, h, 0, 0)),
            pl.BlockSpec((1, nc, 1, CHUNK, LANES), lambda b, h: (b, 0, h, 0, 0)),
            pl.BlockSpec((1, 3, 1, _round_up(nc, 8), LANES), lambda b, h: (b, 0, h, 0, 0)),
        ],
        out_specs=pl.BlockSpec((lp, 2 * LANES), lambda b, h: (b, h)),
        out_shape=jax.ShapeDtypeStruct((bsz * lp, 2 * hk_n * LANES), BF16),
        scratch_shapes=[
            pltpu.VMEM((ls, LANES), BF16),
            pltpu.VMEM((ls, LANES), BF16),
            pltpu.VMEM((2, ls, LANES), BF16),
            pltpu.VMEM((ls, 2 * LANES), F32),
            pltpu.VMEM((nc, 2, 2 * CHUNK, LANES), BF16),
            pltpu.VMEM((nc, CHUNK, 2 * LANES), F32),
            pltpu.VMEM((nc, CHUNK, LANES), BF16),
            pltpu.VMEM((nc, CHUNK, 2 * LANES), BF16),
            pltpu.VMEM((2, HEAD, HEAD), F32),
        ],
        compiler_params=pltpu.CompilerParams(dimension_semantics=("parallel", "arbitrary"),
                                             vmem_limit_bytes=VMEM_LIMIT),
        name="delta_rule",
    )(pj, pj, pj, pj, cw, cw, cw, onw, md, mb, rows)


def _attn_kernel(q_ref, k_ref, v_ref, g_ref, qw_ref, kw_ref, o_ref, qn, kn, vn, acc_s, *, lp, rb, tail):
    lk = kn.shape[0]
    nblk = lp // rb
    scale = HEAD ** -0.5
    if lk > lp:
        zpad = jnp.zeros((lk - lp, HEAD), BF16)
        qn[lp:lk, :] = zpad
        kn[lp:lk, :] = zpad
        vn[lp:lk, :] = zpad
    qw = qw_ref[...]
    kw = kw_ref[...]
    for blk in range(nblk):
        t0 = blk * rb
        x = q_ref[0, t0:t0 + rb, :].astype(F32)
        ms = jnp.mean(x * x, axis=-1, keepdims=True)
        qn[t0:t0 + rb, :] = (x * lax.rsqrt(ms + EPS) * (qw * scale)).astype(BF16)
        x = k_ref[0, t0:t0 + rb, :].astype(F32)
        ms = jnp.mean(x * x, axis=-1, keepdims=True)
        kn[t0:t0 + rb, :] = (x * lax.rsqrt(ms + EPS) * kw).astype(BF16)
        vn[t0:t0 + rb, :] = v_ref[0, t0:t0 + rb, :]

    neg_upper = -(lax.broadcasted_iota(jnp.int32, (K_TILE, K_TILE), 0)
                  > lax.broadcasted_iota(jnp.int32, (K_TILE, K_TILE), 1)).astype(BF16)
    strict_lower = (lax.broadcasted_iota(jnp.int32, (Q_TILE, K_TILE), 1)
                    < lax.broadcasted_iota(jnp.int32, (Q_TILE, K_TILE), 0))

    def softplus(z):
        return jnp.maximum(z, 0.0) + jnp.log(1.0 + jnp.exp(-jnp.abs(z)))

    def run_tiles(q, k0s, first_is_diagonal, acc, cr):
        z_l = [lax.dot_general(q, kn[pl.ds(k0, K_TILE), :], NT_DIMS, preferred_element_type=F32)
               for k0 in k0s]
        zs_l, sp_l = [], []
        for u, z in enumerate(z_l):
            sp = softplus(z)
            zs = z - sp
            if u == 0 and first_is_diagonal:
                zs = jnp.where(strict_lower, zs, MASKED_LOGIT)
                sp = jnp.where(strict_lower, sp, 0.0)
            zs_l.append(zs)
            sp_l.append(sp)
        tl_l = [_dot(sp.astype(BF16), neg_upper) for sp in sp_l]
        a_l = []
        for zs, sp, tl in zip(zs_l, sp_l, tl_l):
            a_l.append(jnp.exp(zs + tl + cr))
            cr = cr + tl[:, 0:1] - sp[:, 0:1]
        for a, k0 in zip(a_l, k0s):
            acc = acc + _dot(a.astype(BF16), vn[pl.ds(k0, K_TILE), :])
        return acc, cr

    def q_loop(qi, carry):
        q0 = pl.multiple_of(qi * Q_TILE, Q_TILE)
        q = qn[pl.ds(q0, Q_TILE), :]
        n_first = qi % ATTN_UNROLL + 1

        def first_block(r):
            def f():
                k0s = [pl.multiple_of((qi - u) * K_TILE, K_TILE) for u in range(r)]
                return run_tiles(q, k0s, True, jnp.zeros((Q_TILE, HEAD), F32), jnp.zeros((Q_TILE, 1), F32))
            return f

        acc, cr = lax.switch(n_first - 1, [first_block(r) for r in range(1, ATTN_UNROLL + 1)])

        def k_loop(j, c2):
            base = qi - n_first - j * ATTN_UNROLL
            k0s = [pl.multiple_of((base - u) * K_TILE, K_TILE) for u in range(ATTN_UNROLL)]
            return run_tiles(q, k0s, False, *c2)

        acc, _ = lax.fori_loop(0, qi // ATTN_UNROLL, k_loop, (acc, cr))
        acc_s[pl.ds(q0, Q_TILE), :] = acc
        return carry

    n_full = lp // Q_TILE
    lax.fori_loop(0, n_full, q_loop, 0)

    q0t = n_full * Q_TILE
    if tail:
        nkt = lk // K_TILE
        qt = qn[q0t:q0t + tail, :]
        z = lax.dot_general(qt, kn[...], NT_DIMS, preferred_element_type=F32)
        sp = softplus(z)
        causal = (lax.broadcasted_iota(jnp.int32, (tail, lk), 1)
                  < q0t + lax.broadcasted_iota(jnp.int32, (tail, lk), 0))
        zs = jnp.where(causal, z - sp, MASKED_LOGIT)
        sp = jnp.where(causal, sp, 0.0)
        spb = sp.astype(BF16)
        tls = [_dot(spb[:, kt * K_TILE:(kt + 1) * K_TILE], neg_upper) for kt in range(nkt)]
        cr = jnp.zeros((tail, 1), F32)
        crs = [None] * nkt
        for kt in reversed(range(nkt)):
            crs[kt] = cr
            cr = cr + tls[kt][:, 0:1] - sp[:, kt * K_TILE:kt * K_TILE + 1]
        acc = jnp.zeros((tail, HEAD), F32)
        for kt in range(nkt):
            sl = slice(kt * K_TILE, (kt + 1) * K_TILE)
            a = jnp.exp(zs[:, sl] + tls[kt] + crs[kt])
            acc = acc + _dot(a.astype(BF16), vn[sl, :])
        acc_s[q0t:q0t + tail, :] = acc
    if lk > q0t + tail:
        acc_s[q0t + tail:lk, :] = jnp.zeros((lk - q0t - tail, HEAD), F32)

    for blk in range(nblk):
        t0 = blk * rb
        g = g_ref[0, t0:t0 + rb, :].astype(F32)
        o_ref[t0:t0 + rb, :] = (acc_s[t0:t0 + rb, :] * (g * jax.nn.sigmoid(g))).astype(o_ref.dtype)


def _sb_attention(pj, qw, kw, *, bsz, lp, l_real, nh):
    lk = _round_up(lp, K_TILE)
    rb = lp // 4 if lp % 64 == 0 and lp >= 1024 else lp
    tail = _round_up(max(l_real - (lp // Q_TILE) * Q_TILE, 0), 16)
    return pl.pallas_call(
        functools.partial(_attn_kernel, lp=lp, rb=rb, tail=tail),
        grid=(bsz, nh),
        in_specs=[
            pl.BlockSpec((1, lp, HEAD), lambda b, h: (h, b, 0)),
            pl.BlockSpec((1, lp, HEAD), lambda b, h: (nh + h, b, 0)),
            pl.BlockSpec((1, lp, HEAD), lambda b, h: (2 * nh + h, b, 0)),
            pl.BlockSpec((1, lp, HEAD), lambda b, h: (3 * nh + h, b, 0)),
            pl.BlockSpec((1, HEAD), lambda b, h: (0, 0)),
            pl.BlockSpec((1, HEAD), lambda b, h: (0, 0)),
        ],
        out_specs=pl.BlockSpec((lp, HEAD), lambda b, h: (b, h)),
        out_shape=jax.ShapeDtypeStruct((bsz * lp, nh * HEAD), BF16),
        scratch_shapes=[pltpu.VMEM((lk, HEAD), BF16),
                        pltpu.VMEM((lk, HEAD), BF16),
                        pltpu.VMEM((lk, HEAD), BF16),
                        pltpu.VMEM((lk, HEAD), F32)],
        compiler_params=pltpu.CompilerParams(dimension_semantics=("parallel", "parallel"),
                                             vmem_limit_bytes=VMEM_LIMIT),
        name="sb_attention",
    )(pj, pj, pj, pj, qw, kw)


def _pick_tile(total, prefs):
    for t in prefs:
        if total % t == 0:
            return t
    return total


def kernel(x, meta_tokens, dn_norm_w, dn_w_in, dn_conv_w, dn_a_log, dn_dt_bias, dn_out_norm_w, dn_w_out,
           sb_norm_w, sb_w_in, sb_q_norm_w, sb_k_norm_w, sb_w_out):
    bsz, seq, d = x.shape
    hk_n = d // HEAD
    hv_n = 2 * hk_n
    assert hv_n == LANES // 2, "gate layout assumes 64 value heads"
    l_real = N_META + seq
    lp = _round_up(l_real, LANES)
    nc = -(-(PAD + l_real) // CHUNK)
    m = bsz * lp
    tm = _pick_tile(m, (768, 512, 384, 256, 128))
    tn_rms = _pick_tile(m, (256, 128))

    meta = jnp.broadcast_to(meta_tokens[None].astype(x.dtype), (bsz, N_META, d))
    h0 = jnp.concatenate([meta, x, jnp.zeros((bsz, lp - l_real, d), x.dtype)], axis=1).reshape(m, d)

    n_main = 2 * hk_n * HEAD + 2 * hv_n * HEAD
    w_main = dn_w_in[0, :, :n_main].astype(BF16)
    perm = jnp.concatenate([jnp.arange(0, hv_n, 2), jnp.arange(1, hv_n, 2)])
    perm = jnp.concatenate([perm, hv_n + perm])
    w_ba = dn_w_in[0, :, n_main:][:, perm].astype(BF16)
    half_perm = perm[:hv_n]
    zeros64 = jnp.zeros((hv_n,), F32)
    alog = jnp.concatenate([zeros64, dn_a_log[0][half_perm].astype(F32)]).reshape(1, LANES)
    dtb = jnp.concatenate([zeros64, dn_dt_bias[0][half_perm].astype(F32)]).reshape(1, LANES)

    hn = _rmsnorm(h0, dn_norm_w[0], tn_rms)
    pj0 = _matmul(hn, w_main, tm=tm, tn=1024, tk=d, head_major=True, out_dtype=BF16, name="dn_in_proj")
    ba = _matmul(hn, w_ba, tm=tm, tn=LANES, tk=d, name="dn_gate_proj")
    md, mb, rows = _gate_prep(ba, alog, dtb, bsz=bsz, lp=lp, nc=nc)
    cw = dn_conv_w[0].astype(F32).reshape(CONV_K, -1, LANES).transpose(1, 0, 2)
    o0 = _delta_rule(pj0, cw, dn_out_norm_w[0].astype(F32).reshape(1, LANES), md, mb, rows,
                     bsz=bsz, lp=lp, nc=nc, hk_n=hk_n)
    h1 = _matmul(o0, dn_w_out[0].astype(BF16), tm=tm, tn=512, tk=2 * d, res=h0, name="dn_out_proj")

    hn1 = _rmsnorm(h1, sb_norm_w[0], tn_rms)
    pj1 = _matmul(hn1, sb_w_in[0].astype(BF16), tm=tm, tn=1024, tk=d, head_major=True, out_dtype=BF16,
                  name="sb_in_proj")
    o1 = _sb_attention(pj1, sb_q_norm_w[0].astype(F32).reshape(1, HEAD),
                       sb_k_norm_w[0].astype(F32).reshape(1, HEAD), bsz=bsz, lp=lp, l_real=l_real, nh=hk_n)
    h2 = _matmul(o1, sb_w_out[0].astype(BF16), tm=tm, tn=1024, tk=d, res=h1, name="sb_out_proj")
    return h2.reshape(bsz, lp, d)[:, N_META:l_real]
```

```python
import functools

import jax
import jax.numpy as jnp
from jax import lax
from jax.experimental import pallas as pl
from jax.experimental.pallas import tpu as pltpu

F32 = jnp.float32
BF16 = jnp.bfloat16

LANES = 128
CHUNK = 64
N_META = 16
PAD = CHUNK - N_META
CONV_K = 4
HEAD = 128
EPS = 1e-6
Q_TILE = 256
K_TILE = 256
VMEM_LIMIT = 56 * 1024 * 1024
MASKED_LOGIT = -1e30
ATTN_UNROLL = 4

NT_DIMS = (((1,), (1,)), ((), ()))
TN_DIMS = (((0,), (0,)), ((), ()))


def _round_up(x, m):
    return -(-x // m) * m


def _unroll_for(n):
    for u in (5, 4, 3, 2):
        if n % u == 0:
            return u
    return 1


def _dot(a, b):
    return jnp.dot(a, b, preferred_element_type=F32)


def _rmsnorm_kernel(x_ref, w_ref, o_ref):
    x = x_ref[...]
    ms = jnp.mean(x * x, axis=-1, keepdims=True)
    o_ref[...] = (x * lax.rsqrt(ms + EPS) * w_ref[...]).astype(o_ref.dtype)


def _rmsnorm(x, w, tm):
    m, d = x.shape
    return pl.pallas_call(
        _rmsnorm_kernel,
        grid=(m // tm,),
        in_specs=[pl.BlockSpec((tm, d), lambda i: (i, 0)),
                  pl.BlockSpec((1, d), lambda i: (0, 0))],
        out_specs=pl.BlockSpec((tm, d), lambda i: (i, 0)),
        out_shape=jax.ShapeDtypeStruct((m, d), BF16),
        compiler_params=pltpu.CompilerParams(dimension_semantics=("parallel",),
                                             vmem_limit_bytes=VMEM_LIMIT),
        name="rmsnorm",
    )(x, w.reshape(1, d).astype(F32))


def _mm_kernel(*refs, nk, head_major, has_res):
    if has_res:
        a_ref, w_ref, r_ref, o_ref = refs[:4]
        acc_ref = refs[4] if nk > 1 else None
    else:
        a_ref, w_ref, o_ref = refs[:3]
        r_ref = None
        acc_ref = refs[3] if nk > 1 else None

    def finish(res):
        if has_res:
            res = res + r_ref[...]
        if head_major:
            for g in range(o_ref.shape[0]):
                o_ref[g] = res[:, g * LANES:(g + 1) * LANES].astype(o_ref.dtype)
        else:
            o_ref[...] = res.astype(o_ref.dtype)

    part = _dot(a_ref[...], w_ref[...])
    if nk == 1:
        finish(part)
    else:
        k = pl.program_id(2)

        @pl.when(k == 0)
        def _():
            acc_ref[...] = part

        @pl.when(jnp.logical_and(k > 0, k < nk - 1))
        def _():
            acc_ref[...] += part

        @pl.when(k == nk - 1)
        def _():
            finish(acc_ref[...] + part)


def _matmul(a, w, *, tm, tn, tk, head_major=False, res=None, out_dtype=F32, name="matmul"):
    m, kdim = a.shape
    n = w.shape[1]
    nk = kdim // tk
    in_specs = [pl.BlockSpec((tm, tk), lambda i, j, k: (i, k)),
                pl.BlockSpec((tk, tn), lambda i, j, k: (k, j))]
    args = [a, w]
    if res is not None:
        in_specs.append(pl.BlockSpec((tm, tn), lambda i, j, k: (i, j)))
        args.append(res)
    if head_major:
        out_spec = pl.BlockSpec((tn // LANES, tm, LANES), lambda i, j, k: (j, i, 0))
        out_shape = jax.ShapeDtypeStruct((n // LANES, m, LANES), out_dtype)
    else:
        out_spec = pl.BlockSpec((tm, tn), lambda i, j, k: (i, j))
        out_shape = jax.ShapeDtypeStruct((m, n), out_dtype)
    scratch = [pltpu.VMEM((tm, tn), F32)] if nk > 1 else []
    return pl.pallas_call(
        functools.partial(_mm_kernel, nk=nk, head_major=head_major, has_res=res is not None),
        grid=(m // tm, n // tn, nk),
        in_specs=in_specs,
        out_specs=out_spec,
        out_shape=out_shape,
        scratch_shapes=scratch,
        compiler_params=pltpu.CompilerParams(
            dimension_semantics=("parallel", "parallel", "arbitrary"),
            vmem_limit_bytes=VMEM_LIMIT),
        name=name,
    )(*args)


def _mm_wcast_kernel(a_ref, w_ref, o_ref, wb_ref):
    @pl.when(pl.program_id(1) == 0)
    def _():
        wb_ref[...] = w_ref[0].astype(BF16)

    res = _dot(a_ref[...], wb_ref[...])
    for g in range(o_ref.shape[0]):
        o_ref[g] = res[:, g * LANES:(g + 1) * LANES].astype(o_ref.dtype)


def _matmul_wcast(a, w3, n, *, tm, tn, name):
    m, kdim = a.shape
    return pl.pallas_call(
        _mm_wcast_kernel,
        grid=(n // tn, m // tm),
        in_specs=[pl.BlockSpec((tm, kdim), lambda j, i: (i, 0)),
                  pl.BlockSpec((1, kdim, tn), lambda j, i: (0, 0, j))],
        out_specs=pl.BlockSpec((tn // LANES, tm, LANES), lambda j, i: (j, i, 0)),
        out_shape=jax.ShapeDtypeStruct((n // LANES, m, LANES), BF16),
        scratch_shapes=[pltpu.VMEM((kdim, tn), BF16)],
        compiler_params=pltpu.CompilerParams(dimension_semantics=("parallel", "arbitrary"),
                                             vmem_limit_bytes=VMEM_LIMIT),
        name=name,
    )(a, w3)


def _gate_kernel(ba_ref, alog_ref, dtb_ref, md_ref, mb_ref, rows_ref, *, npair):
    c = pl.program_id(1)
    half = LANES // 2
    lane = lax.broadcasted_iota(jnp.int32, (CHUNK, LANES), 1)
    sub = lax.broadcasted_iota(jnp.int32, (CHUNK, LANES), 0)
    is_b = lane < half
    jloc = jnp.where(is_b, lane, lane - half)
    low_incl = sub >= jloc
    low_strict = sub > jloc
    tri = (lax.broadcasted_iota(jnp.int32, (CHUNK, CHUNK), 0)
           >= lax.broadcasted_iota(jnp.int32, (CHUNK, CHUNK), 1)).astype(F32)
    neg_a = -jnp.exp(alog_ref[...])
    dtb = dtb_ref[...]

    def one_chunk(x, first):
        beta = jax.nn.sigmoid(x)
        g = neg_a * jax.nn.softplus(x + dtb)
        if first:
            real = sub >= PAD
            beta = jnp.where(real, beta, 0.0)
            g = jnp.where(real, g, 0.0)
        g = jnp.where(is_b, 0.0, g)
        gc = jnp.dot(tri, g, preferred_element_type=F32, precision=lax.Precision.HIGHEST)
        m1 = jnp.where(is_b, beta, gc)
        glb = jnp.broadcast_to(gc[CHUNK - 1:CHUNK, :], (CHUNK, LANES))
        zpad = jnp.zeros((LANES - CHUNK, LANES), F32)
        m1t = jnp.concatenate([m1, zpad], axis=0).T
        glt = jnp.concatenate([glb, zpad], axis=0).T
        q = npair
        beta_p = jnp.concatenate([m1t[0:q, :half], m1t[q:2 * q, :half]], axis=1)
        gc_p = jnp.concatenate([m1t[2 * q:3 * q, :half], m1t[3 * q:4 * q, :half]], axis=1)
        gl_p = jnp.concatenate([glt[2 * q:3 * q, :half], glt[3 * q:4 * q, :half]], axis=1)
        for p in range(npair):
            rows_ref[0, 0, p, pl.ds(c, 1), :] = beta_p[p:p + 1, :]
            rows_ref[0, 1, p, pl.ds(c, 1), :] = gc_p[p:p + 1, :]
            rows_ref[0, 2, p, pl.ds(c, 1), :] = gl_p[p:p + 1, :]
            gcol = jnp.where(is_b, gc[:, 2 * q + p:2 * q + p + 1], gc[:, 3 * q + p:3 * q + p + 1])
            bcol = jnp.where(is_b, beta[:, p:p + 1], beta[:, q + p:q + p + 1])
            dec = jnp.exp(gcol - gc_p[p:p + 1, :])
            md_ref[0, 0, p] = jnp.where(low_incl, dec, 0.0)
            mb_ref[0, 0, p] = jnp.where(low_strict, dec * bcol, 0.0)

    @pl.when(c == 0)
    def _():
        rows_ref[...] = jnp.zeros(rows_ref.shape, F32)
        one_chunk(jnp.concatenate([jnp.zeros((PAD, LANES), F32), ba_ref[0:N_META, :]], axis=0), True)

    @pl.when(c > 0)
    def _():
        r0 = pl.multiple_of(c * CHUNK - PAD, 16)
        one_chunk(ba_ref[pl.ds(r0, CHUNK), :], False)


def _gate_prep(ba, alog, dtb, *, bsz, lp, nc):
    npair = LANES // 4
    ncp = _round_up(nc, 8)
    big = jax.ShapeDtypeStruct((bsz, nc, npair, CHUNK, LANES), F32)
    return pl.pallas_call(
        functools.partial(_gate_kernel, npair=npair),
        grid=(bsz, nc),
        in_specs=[pl.BlockSpec((lp, LANES), lambda b, c: (b, 0)),
                  pl.BlockSpec((1, LANES), lambda b, c: (0, 0)),
                  pl.BlockSpec((1, LANES), lambda b, c: (0, 0))],
        out_specs=[pl.BlockSpec((1, 1, npair, CHUNK, LANES), lambda b, c: (b, c, 0, 0, 0)),
                   pl.BlockSpec((1, 1, npair, CHUNK, LANES), lambda b, c: (b, c, 0, 0, 0)),
                   pl.BlockSpec((1, 3, npair, ncp, LANES), lambda b, c: (b, 0, 0, 0, 0))],
        out_shape=[big, big, jax.ShapeDtypeStruct((bsz, 3, npair, ncp, LANES), F32)],
        compiler_params=pltpu.CompilerParams(dimension_semantics=("parallel", "arbitrary"),
                                             vmem_limit_bytes=VMEM_LIMIT),
        name="gate_prep",
    )(ba, alog, dtb)


def _delta_kernel(q_ref, k_ref, v_ref, z_ref, cwq_ref, cwk_ref, cwv_ref, onw_ref,
                  md_ref, mb_ref, rows_ref, o_ref,
                  qs, ks, vs, ob, wq_s, u_s, att_s, kd_s, st_s, *, lp, nc, rb):
    half = LANES // 2
    ls = qs.shape[0]
    nblk = lp // rb

    ob[0:16, 0:LANES] = jnp.zeros((16, LANES), F32)

    def conv_slab(load, w, store, normalize, scale):
        for blk in range(nblk):
            t0 = blk * rb
            ob[16 + t0:16 + t0 + rb, 0:LANES] = load(t0).astype(F32)
        for blk in range(nblk):
            t0 = blk * rb
            y = w[CONV_K - 1:CONV_K, :] * ob[16 + t0:16 + t0 + rb, 0:LANES]
            for j in range(CONV_K - 1):
                off = 16 + t0 - (CONV_K - 1) + j
                y = y + w[j:j + 1, :] * ob[off:off + rb, 0:LANES]
            y = y * jax.nn.sigmoid(y)
            if normalize:
                ss = jnp.sum(y * y, axis=-1, keepdims=True)
                y = y * (lax.rsqrt(ss + EPS) * scale)
            store(t0, y.astype(BF16))

    def store_to(ref):
        def f(t0, y):
            ref[PAD + t0:PAD + t0 + rb, :] = y
        return f

    def store_v(h):
        def f(t0, y):
            vs[h, PAD + t0:PAD + t0 + rb, :] = y
        return f

    zero_front = jnp.zeros((PAD, LANES), BF16)
    qs[0:PAD, :] = zero_front
    ks[0:PAD, :] = zero_front
    vs[0, 0:PAD, :] = zero_front
    vs[1, 0:PAD, :] = zero_front
    if ls > PAD + lp:
        tail = jnp.zeros((ls - PAD - lp, LANES), BF16)
        qs[PAD + lp:ls, :] = tail
        ks[PAD + lp:ls, :] = tail
        vs[0, PAD + lp:ls, :] = tail
        vs[1, PAD + lp:ls, :] = tail
    conv_slab(lambda t0: q_ref[0, t0:t0 + rb, :], cwq_ref[0], store_to(qs), True, HEAD ** -0.5)
    conv_slab(lambda t0: k_ref[0, t0:t0 + rb, :], cwk_ref[0], store_to(ks), True, 1.0)
    conv_slab(lambda t0: v_ref[0, t0:t0 + rb, :], cwv_ref[0], store_v(0), False, 1.0)
    conv_slab(lambda t0: v_ref[1, t0:t0 + rb, :], cwv_ref[1], store_v(1), False, 1.0)

    lane = lax.broadcasted_iota(jnp.int32, (CHUNK, LANES), 1)
    sub = lax.broadcasted_iota(jnp.int32, (CHUNK, LANES), 0)
    left = lane < half
    eye2 = (jnp.where(left, lane, lane - half) == sub).astype(F32)
    zero_b = jnp.zeros((CHUNK, LANES), BF16)
    zero_f = jnp.zeros((CHUNK, LANES), F32)

    def blockdiag(p):
        return jnp.concatenate([jnp.where(left, p, zero_f), jnp.where(left, zero_f, p)],
                               axis=0).astype(BF16)

    def blockdiag_wide(a, b):
        return jnp.concatenate([jnp.concatenate([a, zero_b], axis=1),
                                jnp.concatenate([zero_b, b], axis=1)], axis=0)

    grp = _unroll_for(nc)
    ngrp = nc // grp
    st_s[...] = jnp.zeros(st_s.shape, F32)
    if ls > nc * CHUNK:
        ob[nc * CHUNK:ls, :] = jnp.zeros((ls - nc * CHUNK, 2 * LANES), F32)
    left8 = lax.broadcasted_iota(jnp.int32, (8, LANES), 1) < half

    def local_steps(i):
        cs = [i * grp + g for g in range(grp)]
        r0s = [pl.multiple_of(c * CHUNK, CHUNK) for c in cs]
        v = {}

        def gram():
            v["kb"] = [ks[pl.ds(r0, CHUNK), :] for r0 in r0s]
            v["qb"] = [qs[pl.ds(r0, CHUNK), :] for r0 in r0s]
            a2s = [lax.dot_general(jnp.concatenate([kb, qb], axis=0), jnp.concatenate([kb, kb], axis=0),
                                   NT_DIMS, preferred_element_type=F32)
                   for kb, qb in zip(v["kb"], v["qb"])]
            v["n"] = [-(a2[:CHUNK] * mb_ref[0, c, 0]) for a2, c in zip(a2s, cs)]
            v["att"] = [a2[CHUNK:] * md_ref[0, c, 0] for a2, c in zip(a2s, cs)]

        def square_first():
            v["p"] = [_dot(n.astype(BF16), blockdiag(n)) for n in v["n"]]
            v["t"] = [eye2 + n for n in v["n"]]

        def square_next():
            rs = [_dot(jnp.concatenate([p, t], axis=0).astype(BF16), blockdiag(p))
                  for p, t in zip(v["p"], v["t"])]
            v["p"] = [r[:CHUNK] for r in rs]
            v["t"] = [t + r[CHUNK:] for t, r in zip(v["t"], rs)]

        def square_last():
            v["t"] = [t + _dot(t.astype(BF16), blockdiag(p)) for t, p in zip(v["t"], v["p"])]
            v["beta"] = [rows_ref[0, 0, 0, pl.ds(c, 1), :] for c in cs]
            gc_ps = [rows_ref[0, 1, 0, pl.ds(c, 1), :] for c in cs]
            gl_ps = [rows_ref[0, 2, 0, pl.ds(c, 1), :] for c in cs]
            v["egc"] = [jnp.exp(gc_p) for gc_p in gc_ps]
            v["ekd"] = [jnp.exp(gl_p - gc_p) for gl_p, gc_p in zip(gl_ps, gc_ps)]

        def make_u():
            u2s = [_dot((t * beta_p).astype(BF16),
                        blockdiag_wide(vs[0, pl.ds(r0, CHUNK), :], vs[1, pl.ds(r0, CHUNK), :]))
                   for t, beta_p, r0 in zip(v["t"], v["beta"], r0s)]
            for c, u2 in zip(cs, u2s):
                u_s[c] = u2

        def make_w():
            v["wk"] = [_dot(jnp.concatenate([(t * (beta_p * egc)).astype(BF16),
                                             (eye2 * ekd).astype(BF16)], axis=0),
                            blockdiag_wide(kb, kb))
                       for t, beta_p, egc, ekd, kb in zip(v["t"], v["beta"], v["egc"], v["ekd"], v["kb"])]

        def make_qd():
            qd2s = [_dot((eye2 * egc).astype(BF16), blockdiag_wide(qb, qb))
                    for egc, qb in zip(v["egc"], v["qb"])]
            for g, c in enumerate(cs):
                for h in range(2):
                    sl = slice(h * LANES, (h + 1) * LANES)
                    wq_s[c, h] = jnp.concatenate([v["wk"][g][:CHUNK, sl], qd2s[g][:, sl]],
                                                 axis=0).astype(BF16)
                att_s[c] = v["att"][g].astype(BF16)
                kd_s[c] = v["wk"][g][CHUNK:].astype(BF16)

        return [gram, square_first, square_next, square_next, square_next, square_next, square_last,
                make_u, make_w, make_qd]

    def state_steps(i):
        steps = []
        for g in range(grp):
            c = i * grp + g
            v = {}

            def read_state(c=c, v=v):
                u2 = u_s[c]
                v["r"] = [_dot(wq_s[c, h], st_s[h].astype(BF16)) for h in range(2)]
                v["vn"] = [(u2[:, h * LANES:(h + 1) * LANES] - v["r"][h][:CHUNK]).astype(BF16)
                           for h in range(2)]

            def write_state(c=c, v=v):
                r0 = pl.multiple_of(c * CHUNK, CHUNK)
                egl = jnp.broadcast_to(jnp.exp(rows_ref[0, 2, 0, pl.ds(c, 1), :]), (8, LANES))
                egl_sw = pltpu.roll(egl, half, 1)
                egl_h = (jnp.where(left8, egl, egl_sw)[0:1], jnp.where(left8, egl_sw, egl)[0:1])
                o2 = _dot(att_s[c], blockdiag_wide(v["vn"][0], v["vn"][1]))
                ob[pl.ds(r0, CHUNK), :] = o2 + jnp.concatenate([v["r"][0][CHUNK:], v["r"][1][CHUNK:]],
                                                              axis=1)
                kd2 = kd_s[c]
                for h in range(2):
                    upd = lax.dot_general(kd2[:, h * LANES:(h + 1) * LANES], v["vn"][h], TN_DIMS,
                                          preferred_element_type=F32)
                    st_s[h] = st_s[h] * egl_h[h] + upd

            steps += [read_state, write_state]
        return steps

    def run_interleaved(first, second):
        n1, n2 = len(first), len(second)
        j = 0
        for k, step in enumerate(first):
            step()
            while j < n2 and (j + 1) * n1 <= (k + 1) * n2:
                second[j]()
                j += 1

    run_interleaved(local_steps(jnp.int32(0)), [])

    def body(i, carry):
        run_interleaved(state_steps(i), local_steps(i + 1))
        return carry

    lax.fori_loop(0, ngrp - 1, body, 0)
    run_interleaved(state_steps(jnp.int32(ngrp - 1)), [])

    onw = onw_ref[...]
    for blk in range(nblk):
        t0 = blk * rb
        for h in range(2):
            o = ob[PAD + t0:PAD + t0 + rb, h * LANES:(h + 1) * LANES]
            ms = jnp.mean(o * o, axis=-1, keepdims=True)
            zz = z_ref[h, t0:t0 + rb, :].astype(F32)
            o_ref[t0:t0 + rb, h * LANES:(h + 1) * LANES] = (
                o * lax.rsqrt(ms + EPS) * onw * (zz * jax.nn.sigmoid(zz))).astype(o_ref.dtype)


def _delta_rule(pj, cw, onw, md, mb, rows, *, bsz, lp, nc, hk_n):
    ls = max(PAD + lp, nc * CHUNK)
    rb = lp // 4 if lp % 32 == 0 and lp >= 1024 else lp
    kern = functools.partial(_delta_kernel, lp=lp, nc=nc, rb=rb)
    return pl.pallas_call(
        kern,
        grid=(bsz, hk_n),
        in_specs=[
            pl.BlockSpec((1, lp, LANES), lambda b, h: (h, b, 0)),
            pl.BlockSpec((1, lp, LANES), lambda b, h: (hk_n + h, b, 0)),
            pl.BlockSpec((2, lp, LANES), lambda b, h: (hk_n + h, b, 0)),
            pl.BlockSpec((2, lp, LANES), lambda b, h: (2 * hk_n + h, b, 0)),
            pl.BlockSpec((1, CONV_K, LANES), lambda b, h: (h, 0, 0)),
            pl.BlockSpec((1, CONV_K, LANES), lambda b, h: (hk_n + h, 0, 0)),
            pl.BlockSpec((2, CONV_K, LANES), lambda b, h: (hk_n + h, 0, 0)),
            pl.BlockSpec((1, LANES), lambda b, h: (0, 0)),
            pl.BlockSpec((1, nc, 1, CHUNK, LANES), lambda b, h: (b, 0, h, 0, 0)),
            pl.BlockSpec((1, nc, 1, CHUNK, LANES), lambda b, h: (b, 0, h, 0, 0)),
            pl.BlockSpec((1, 3, 1, _round_up(nc, 8), LANES), lambda b, h: (b, 0, h, 0, 0)),
        ],
        out_specs=pl.BlockSpec((lp, 2 * LANES), lambda b, h: (b, h)),
        out_shape=jax.ShapeDtypeStruct((bsz * lp, 2 * hk_n * LANES), BF16),
        scratch_shapes=[
            pltpu.VMEM((ls, LANES), BF16),
            pltpu.VMEM((ls, LANES), BF16),
            pltpu.VMEM((2, ls, LANES), BF16),
            pltpu.VMEM((ls, 2 * LANES), F32),
            pltpu.VMEM((nc, 2, 2 * CHUNK, LANES), BF16),
            pltpu.VMEM((nc, CHUNK, 2 * LANES), F32),
            pltpu.VMEM((nc, CHUNK, LANES), BF16),
            pltpu.VMEM((nc, CHUNK, 2 * LANES), BF16),
            pltpu.VMEM((2, HEAD, HEAD), F32),
        ],
        compiler_params=pltpu.CompilerParams(dimension_semantics=("parallel", "arbitrary"),
                                             vmem_limit_bytes=VMEM_LIMIT),
        name="delta_rule",
    )(pj, pj, pj, pj, cw, cw, cw, onw, md, mb, rows)


def _attn_kernel(q_ref, k_ref, v_ref, g_ref, qw_ref, kw_ref, o_ref, qn, kn, vn, acc_s, *, lp, rb, tail):
    lk = kn.shape[0]
    nblk = lp // rb
    scale = HEAD ** -0.5
    if lk > lp:
        zpad = jnp.zeros((lk - lp, HEAD), BF16)
        qn[lp:lk, :] = zpad
        kn[lp:lk, :] = zpad
        vn[lp:lk, :] = zpad
    qw = qw_ref[...]
    kw = kw_ref[...]
    for blk in range(nblk):
        t0 = blk * rb
        x = q_ref[0, t0:t0 + rb, :].astype(F32)
        ms = jnp.mean(x * x, axis=-1, keepdims=True)
        qn[t0:t0 + rb, :] = (x * lax.rsqrt(ms + EPS) * (qw * scale)).astype(BF16)
        x = k_ref[0, t0:t0 + rb, :].astype(F32)
        ms = jnp.mean(x * x, axis=-1, keepdims=True)
        kn[t0:t0 + rb, :] = (x * lax.rsqrt(ms + EPS) * kw).astype(BF16)
        vn[t0:t0 + rb, :] = v_ref[0, t0:t0 + rb, :]

    neg_upper = -(lax.broadcasted_iota(jnp.int32, (K_TILE, K_TILE), 0)
                  > lax.broadcasted_iota(jnp.int32, (K_TILE, K_TILE), 1)).astype(BF16)
    strict_lower = (lax.broadcasted_iota(jnp.int32, (Q_TILE, K_TILE), 1)
                    < lax.broadcasted_iota(jnp.int32, (Q_TILE, K_TILE), 0))

    def softplus(z):
        return jnp.maximum(z, 0.0) + jnp.log(1.0 + jnp.exp(-jnp.abs(z)))

    def run_tiles(q, k0s, first_is_diagonal, acc, cr):
        z_l = [lax.dot_general(q, kn[pl.ds(k0, K_TILE), :], NT_DIMS, preferred_element_type=F32)
               for k0 in k0s]
        zs_l, spb_l, sp0_l = [], [], []
        for u, z in enumerate(z_l):
            sp = softplus(z)
            zs = z - sp
            if u == 0 and first_is_diagonal:
                zs = jnp.where(strict_lower, zs, MASKED_LOGIT)
                sp = jnp.where(strict_lower, sp, 0.0)
            zs_l.append(zs)
            spb_l.append(sp.astype(BF16))
            sp0_l.append(sp[:, 0:1])
        tl_l = [_dot(spb, neg_upper) for spb in spb_l]
        a_l = []
        for zs, sp0, tl in zip(zs_l, sp0_l, tl_l):
            a_l.append(jnp.exp(zs + tl + cr))
            cr = cr + tl[:, 0:1] - sp0
        for a, k0 in zip(a_l, k0s):
            acc = acc + _dot(a.astype(BF16), vn[pl.ds(k0, K_TILE), :])
        return acc, cr

    def q_loop(qi, carry):
        q0 = pl.multiple_of(qi * Q_TILE, Q_TILE)
        q = qn[pl.ds(q0, Q_TILE), :]
        n_loop = jnp.maximum(qi // ATTN_UNROLL - 1, 0)
        n_first = qi + 1 - n_loop * ATTN_UNROLL

        def first_block(r):
            def f():
                k0s = [pl.multiple_of((qi - u) * K_TILE, K_TILE) for u in range(r)]
                return run_tiles(q, k0s, True, jnp.zeros((Q_TILE, HEAD), F32), jnp.zeros((Q_TILE, 1), F32))
            return f

        acc, cr = lax.switch(n_first - 1, [first_block(r) for r in range(1, 2 * ATTN_UNROLL + 1)])

        def k_loop(j, c2):
            base = qi - n_first - j * ATTN_UNROLL
            k0s = [pl.multiple_of((base - u) * K_TILE, K_TILE) for u in range(ATTN_UNROLL)]
            return run_tiles(q, k0s, False, *c2)

        acc, _ = lax.fori_loop(0, n_loop, k_loop, (acc, cr))
        acc_s[pl.ds(q0, Q_TILE), :] = acc
        return carry

    n_full = lp // Q_TILE
    lax.fori_loop(0, n_full, q_loop, 0)

    q0t = n_full * Q_TILE
    if tail:
        nkt = lk // K_TILE
        qt = qn[q0t:q0t + tail, :]
        z = lax.dot_general(qt, kn[...], NT_DIMS, preferred_element_type=F32)
        sp = softplus(z)
        causal = (lax.broadcasted_iota(jnp.int32, (tail, lk), 1)
                  < q0t + lax.broadcasted_iota(jnp.int32, (tail, lk), 0))
        zs = jnp.where(causal, z - sp, MASKED_LOGIT)
        sp = jnp.where(causal, sp, 0.0)
        spb = sp.astype(BF16)
        tls = [_dot(spb[:, kt * K_TILE:(kt + 1) * K_TILE], neg_upper) for kt in range(nkt)]
        cr = jnp.zeros((tail, 1), F32)
        crs = [None] * nkt
        for kt in reversed(range(nkt)):
            crs[kt] = cr
            cr = cr + tls[kt][:, 0:1] - sp[:, kt * K_TILE:kt * K_TILE + 1]
        acc = jnp.zeros((tail, HEAD), F32)
        for kt in range(nkt):
            sl = slice(kt * K_TILE, (kt + 1) * K_TILE)
            a = jnp.exp(zs[:, sl] + tls[kt] + crs[kt])
            acc = acc + _dot(a.astype(BF16), vn[sl, :])
        acc_s[q0t:q0t + tail, :] = acc
    if lk > q0t + tail:
        acc_s[q0t + tail:lk, :] = jnp.zeros((lk - q0t - tail, HEAD), F32)

    for blk in range(nblk):
        t0 = blk * rb
        g = g_ref[0, t0:t0 + rb, :].astype(F32)
        o_ref[t0:t0 + rb, :] = (acc_s[t0:t0 + rb, :] * (g * jax.nn.sigmoid(g))).astype(o_ref.dtype)


def _sb_attention(pj, qw, kw, *, bsz, lp, l_real, nh):
    lk = _round_up(lp, K_TILE)
    rb = lp // 4 if lp % 64 == 0 and lp >= 1024 else lp
    tail = _round_up(max(l_real - (lp // Q_TILE) * Q_TILE, 0), 16)
    return pl.pallas_call(
        functools.partial(_attn_kernel, lp=lp, rb=rb, tail=tail),
        grid=(bsz, nh),
        in_specs=[
            pl.BlockSpec((1, lp, HEAD), lambda b, h: (h, b, 0)),
            pl.BlockSpec((1, lp, HEAD), lambda b, h: (nh + h, b, 0)),
            pl.BlockSpec((1, lp, HEAD), lambda b, h: (2 * nh + h, b, 0)),
            pl.BlockSpec((1, lp, HEAD), lambda b, h: (3 * nh + h, b, 0)),
            pl.BlockSpec((1, HEAD), lambda b, h: (0, 0)),
            pl.BlockSpec((1, HEAD), lambda b, h: (0, 0)),
        ],
        out_specs=pl.BlockSpec((lp, HEAD), lambda b, h: (b, h)),
        out_shape=jax.ShapeDtypeStruct((bsz * lp, nh * HEAD), BF16),
        scratch_shapes=[pltpu.VMEM((lk, HEAD), BF16),
                        pltpu.VMEM((lk, HEAD), BF16),
                        pltpu.VMEM((lk, HEAD), BF16),
                        pltpu.VMEM((lk, HEAD), F32)],
        compiler_params=pltpu.CompilerParams(dimension_semantics=("parallel", "parallel"),
                                             vmem_limit_bytes=VMEM_LIMIT),
        name="sb_attention",
    )(pj, pj, pj, pj, qw, kw)


def _pick_tile(total, prefs):
    for t in prefs:
        if total % t == 0:
            return t
    return total


def kernel(x, meta_tokens, dn_norm_w, dn_w_in, dn_conv_w, dn_a_log, dn_dt_bias, dn_out_norm_w, dn_w_out,
           sb_norm_w, sb_w_in, sb_q_norm_w, sb_k_norm_w, sb_w_out):
    bsz, seq, d = x.shape
    hk_n = d // HEAD
    hv_n = 2 * hk_n
    assert hv_n == LANES // 2, "gate layout assumes 64 value heads"
    l_real = N_META + seq
    lp = _round_up(l_real, LANES)
    nc = -(-(PAD + l_real) // CHUNK)
    m = bsz * lp
    tm = _pick_tile(m, (768, 512, 384, 256, 128))
    tn_rms = _pick_tile(m, (256, 128))

    meta = jnp.broadcast_to(meta_tokens[None].astype(x.dtype), (bsz, N_META, d))
    h0 = jnp.concatenate([meta, x, jnp.zeros((bsz, lp - l_real, d), x.dtype)], axis=1).reshape(m, d)

    n_main = 2 * hk_n * HEAD + 2 * hv_n * HEAD
    perm = jnp.concatenate([jnp.arange(0, hv_n, 2), jnp.arange(1, hv_n, 2)])
    perm = jnp.concatenate([perm, hv_n + perm])
    w_ba = dn_w_in[0, :, n_main:][:, perm].astype(BF16)
    half_perm = perm[:hv_n]
    zeros64 = jnp.zeros((hv_n,), F32)
    alog = jnp.concatenate([zeros64, dn_a_log[0][half_perm].astype(F32)]).reshape(1, LANES)
    dtb = jnp.concatenate([zeros64, dn_dt_bias[0][half_perm].astype(F32)]).reshape(1, LANES)

    hn = _rmsnorm(h0, dn_norm_w[0], tn_rms)
    pj0 = _matmul_wcast(hn, dn_w_in, n_main, tm=tm, tn=_pick_tile(n_main, (768, 512, 256)), name="dn_in_proj")
    ba = _matmul(hn, w_ba, tm=tm, tn=LANES, tk=d, name="dn_gate_proj")
    md, mb, rows = _gate_prep(ba, alog, dtb, bsz=bsz, lp=lp, nc=nc)
    cw = dn_conv_w[0].astype(F32).reshape(CONV_K, -1, LANES).transpose(1, 0, 2)
    o0 = _delta_rule(pj0, cw, dn_out_norm_w[0].astype(F32).reshape(1, LANES), md, mb, rows,
                     bsz=bsz, lp=lp, nc=nc, hk_n=hk_n)
    h1 = _matmul(o0, dn_w_out[0].astype(BF16), tm=tm, tn=512, tk=2 * d, res=h0, name="dn_out_proj")

    hn1 = _rmsnorm(h1, sb_norm_w[0], tn_rms)
    pj1 = _matmul_wcast(hn1, sb_w_in, sb_w_in.shape[-1], tm=tm, tn=512, name="sb_in_proj")
    o1 = _sb_attention(pj1, sb_q_norm_w[0].astype(F32).reshape(1, HEAD),
                       sb_k_norm_w[0].astype(F32).reshape(1, HEAD), bsz=bsz, lp=lp, l_real=l_real, nh=hk_n)
    h2 = _matmul(o1, sb_w_out[0].astype(BF16), tm=tm, tn=1024, tk=d, res=h1, name="sb_out_proj")
    return h2.reshape(bsz, lp, d)[:, N_META:l_real]
```

```python
import functools

import jax
import jax.numpy as jnp
from jax import lax
from jax.experimental import pallas as pl
from jax.experimental.pallas import tpu as pltpu

F32 = jnp.float32
BF16 = jnp.bfloat16

LANES = 128
CHUNK = 64
N_META = 16
PAD = CHUNK - N_META
CONV_K = 4
HEAD = 128
EPS = 1e-6
Q_TILE = 256
K_TILE = 256
VMEM_LIMIT = 56 * 1024 * 1024
MASKED_LOGIT = -1e30
ATTN_UNROLL = 4

NT_DIMS = (((1,), (1,)), ((), ()))
TN_DIMS = (((0,), (0,)), ((), ()))


def _round_up(x, m):
    return -(-x // m) * m


def _unroll_for(n):
    for u in (5, 4, 3, 2):
        if n % u == 0:
            return u
    return 1


def _dot(a, b):
    return jnp.dot(a, b, preferred_element_type=F32)


def _rmsnorm_kernel(x_ref, w_ref, o_ref):
    x = x_ref[...]
    ms = jnp.mean(x * x, axis=-1, keepdims=True)
    o_ref[...] = (x * lax.rsqrt(ms + EPS) * w_ref[...]).astype(o_ref.dtype)


def _rmsnorm(x, w, tm):
    m, d = x.shape
    return pl.pallas_call(
        _rmsnorm_kernel,
        grid=(m // tm,),
        in_specs=[pl.BlockSpec((tm, d), lambda i: (i, 0)),
                  pl.BlockSpec((1, d), lambda i: (0, 0))],
        out_specs=pl.BlockSpec((tm, d), lambda i: (i, 0)),
        out_shape=jax.ShapeDtypeStruct((m, d), BF16),
        compiler_params=pltpu.CompilerParams(dimension_semantics=("parallel",),
                                             vmem_limit_bytes=VMEM_LIMIT),
        name="rmsnorm",
    )(x, w.reshape(1, d).astype(F32))


def _mm_kernel(*refs, nk, head_major, has_res):
    if has_res:
        a_ref, w_ref, r_ref, o_ref = refs[:4]
        acc_ref = refs[4] if nk > 1 else None
    else:
        a_ref, w_ref, o_ref = refs[:3]
        r_ref = None
        acc_ref = refs[3] if nk > 1 else None

    def finish(res):
        if has_res:
            res = res + r_ref[...]
        if head_major:
            for g in range(o_ref.shape[0]):
                o_ref[g] = res[:, g * LANES:(g + 1) * LANES].astype(o_ref.dtype)
        else:
            o_ref[...] = res.astype(o_ref.dtype)

    part = _dot(a_ref[...], w_ref[...])
    if nk == 1:
        finish(part)
    else:
        k = pl.program_id(2)

        @pl.when(k == 0)
        def _():
            acc_ref[...] = part

        @pl.when(jnp.logical_and(k > 0, k < nk - 1))
        def _():
            acc_ref[...] += part

        @pl.when(k == nk - 1)
        def _():
            finish(acc_ref[...] + part)


def _matmul(a, w, *, tm, tn, tk, head_major=False, res=None, out_dtype=F32, name="matmul"):
    m, kdim = a.shape
    n = w.shape[1]
    nk = kdim // tk
    in_specs = [pl.BlockSpec((tm, tk), lambda i, j, k: (i, k)),
                pl.BlockSpec((tk, tn), lambda i, j, k: (k, j))]
    args = [a, w]
    if res is not None:
        in_specs.append(pl.BlockSpec((tm, tn), lambda i, j, k: (i, j)))
        args.append(res)
    if head_major:
        out_spec = pl.BlockSpec((tn // LANES, tm, LANES), lambda i, j, k: (j, i, 0))
        out_shape = jax.ShapeDtypeStruct((n // LANES, m, LANES), out_dtype)
    else:
        out_spec = pl.BlockSpec((tm, tn), lambda i, j, k: (i, j))
        out_shape = jax.ShapeDtypeStruct((m, n), out_dtype)
    scratch = [pltpu.VMEM((tm, tn), F32)] if nk > 1 else []
    return pl.pallas_call(
        functools.partial(_mm_kernel, nk=nk, head_major=head_major, has_res=res is not None),
        grid=(m // tm, n // tn, nk),
        in_specs=in_specs,
        out_specs=out_spec,
        out_shape=out_shape,
        scratch_shapes=scratch,
        compiler_params=pltpu.CompilerParams(
            dimension_semantics=("parallel", "parallel", "arbitrary"),
            vmem_limit_bytes=VMEM_LIMIT),
        name=name,
    )(*args)


def _mm_wcast_kernel(a_ref, w_ref, o_ref, wb_ref):
    @pl.when(pl.program_id(1) == 0)
    def _():
        wb_ref[...] = w_ref[0].astype(BF16)

    res = _dot(a_ref[...], wb_ref[...])
    for g in range(o_ref.shape[0]):
        o_ref[g] = res[:, g * LANES:(g + 1) * LANES].astype(o_ref.dtype)


def _matmul_wcast(a, w3, n, *, tm, tn, name):
    m, kdim = a.shape
    return pl.pallas_call(
        _mm_wcast_kernel,
        grid=(n // tn, m // tm),
        in_specs=[pl.BlockSpec((tm, kdim), lambda j, i: (i, 0)),
                  pl.BlockSpec((1, kdim, tn), lambda j, i: (0, 0, j))],
        out_specs=pl.BlockSpec((tn // LANES, tm, LANES), lambda j, i: (j, i, 0)),
        out_shape=jax.ShapeDtypeStruct((n // LANES, m, LANES), BF16),
        scratch_shapes=[pltpu.VMEM((kdim, tn), BF16)],
        compiler_params=pltpu.CompilerParams(dimension_semantics=("parallel", "arbitrary"),
                                             vmem_limit_bytes=VMEM_LIMIT),
        name=name,
    )(a, w3)


def _gate_kernel(ba_ref, alog_ref, dtb_ref, md_ref, mb_ref, rows_ref, *, npair):
    c = pl.program_id(1)
    half = LANES // 2
    lane = lax.broadcasted_iota(jnp.int32, (CHUNK, LANES), 1)
    sub = lax.broadcasted_iota(jnp.int32, (CHUNK, LANES), 0)
    is_b = lane < half
    jloc = jnp.where(is_b, lane, lane - half)
    low_incl = sub >= jloc
    low_strict = sub > jloc
    tri = (lax.broadcasted_iota(jnp.int32, (CHUNK, CHUNK), 0)
           >= lax.broadcasted_iota(jnp.int32, (CHUNK, CHUNK), 1)).astype(F32)
    neg_a = -jnp.exp(alog_ref[...])
    dtb = dtb_ref[...]

    def one_chunk(x, first):
        beta = jax.nn.sigmoid(x)
        g = neg_a * jax.nn.softplus(x + dtb)
        if first:
            real = sub >= PAD
            beta = jnp.where(real, beta, 0.0)
            g = jnp.where(real, g, 0.0)
        g = jnp.where(is_b, 0.0, g)
        gc = jnp.dot(tri, g, preferred_element_type=F32, precision=lax.Precision.HIGHEST)
        m1 = jnp.where(is_b, beta, gc)
        glb = jnp.broadcast_to(gc[CHUNK - 1:CHUNK, :], (CHUNK, LANES))
        zpad = jnp.zeros((LANES - CHUNK, LANES), F32)
        m1t = jnp.concatenate([m1, zpad], axis=0).T
        glt = jnp.concatenate([glb, zpad], axis=0).T
        q = npair
        beta_p = jnp.concatenate([m1t[0:q, :half], m1t[q:2 * q, :half]], axis=1)
        gc_p = jnp.concatenate([m1t[2 * q:3 * q, :half], m1t[3 * q:4 * q, :half]], axis=1)
        gl_p = jnp.concatenate([glt[2 * q:3 * q, :half], glt[3 * q:4 * q, :half]], axis=1)
        for p in range(npair):
            rows_ref[0, 0, p, pl.ds(c, 1), :] = beta_p[p:p + 1, :]
            rows_ref[0, 1, p, pl.ds(c, 1), :] = gc_p[p:p + 1, :]
            rows_ref[0, 2, p, pl.ds(c, 1), :] = gl_p[p:p + 1, :]
            gcol = jnp.where(is_b, gc[:, 2 * q + p:2 * q + p + 1], gc[:, 3 * q + p:3 * q + p + 1])
            bcol = jnp.where(is_b, beta[:, p:p + 1], beta[:, q + p:q + p + 1])
            dec = jnp.exp(gcol - gc_p[p:p + 1, :])
            md_ref[0, 0, p] = jnp.where(low_incl, dec, 0.0)
            mb_ref[0, 0, p] = jnp.where(low_strict, dec * bcol, 0.0)

    @pl.when(c == 0)
    def _():
        rows_ref[...] = jnp.zeros(rows_ref.shape, F32)
        one_chunk(jnp.concatenate([jnp.zeros((PAD, LANES), F32), ba_ref[0:N_META, :]], axis=0), True)

    @pl.when(c > 0)
    def _():
        r0 = pl.multiple_of(c * CHUNK - PAD, 16)
        one_chunk(ba_ref[pl.ds(r0, CHUNK), :], False)


def _gate_prep(ba, alog, dtb, *, bsz, lp, nc):
    npair = LANES // 4
    ncp = _round_up(nc, 8)
    big = jax.ShapeDtypeStruct((bsz, nc, npair, CHUNK, LANES), F32)
    return pl.pallas_call(
        functools.partial(_gate_kernel, npair=npair),
        grid=(bsz, nc),
        in_specs=[pl.BlockSpec((lp, LANES), lambda b, c: (b, 0)),
                  pl.BlockSpec((1, LANES), lambda b, c: (0, 0)),
                  pl.BlockSpec((1, LANES), lambda b, c: (0, 0))],
        out_specs=[pl.BlockSpec((1, 1, npair, CHUNK, LANES), lambda b, c: (b, c, 0, 0, 0)),
                   pl.BlockSpec((1, 1, npair, CHUNK, LANES), lambda b, c: (b, c, 0, 0, 0)),
                   pl.BlockSpec((1, 3, npair, ncp, LANES), lambda b, c: (b, 0, 0, 0, 0))],
        out_shape=[big, big, jax.ShapeDtypeStruct((bsz, 3, npair, ncp, LANES), F32)],
        compiler_params=pltpu.CompilerParams(dimension_semantics=("parallel", "arbitrary"),
                                             vmem_limit_bytes=VMEM_LIMIT),
        name="gate_prep",
    )(ba, alog, dtb)


def _delta_kernel(q_ref, k_ref, v_ref, z_ref, cwq_ref, cwk_ref, cwv_ref, onw_ref,
                  md_ref, mb_ref, rows_ref, o_ref,
                  qs, ks, vs, ob, wq_s, u_s, att_s, kd_s, st_s, cs, *, lp, nc, rb):
    half = LANES // 2
    ls = qs.shape[0]
    nblk = lp // rb

    grp = _unroll_for(nc)
    ngrp = nc // grp
    gr = grp * CHUNK
    conv_srcs = (lambda sl: q_ref[0, sl, :], lambda sl: k_ref[0, sl, :],
                 lambda sl: v_ref[0, sl, :], lambda sl: v_ref[1, sl, :])
    conv_ws = (cwq_ref[0], cwk_ref[0], cwv_ref[0], cwv_ref[1])

    def conv_group(g, first):
        for s in range(4):
            if first:
                cs[s, :, :] = jnp.concatenate(
                    [jnp.zeros((16 + PAD, LANES), F32), conv_srcs[s](slice(0, gr - PAD)).astype(F32)], axis=0)
                rows = slice(0, gr)
            else:
                t0 = pl.multiple_of(g * gr - PAD - 16, 16)
                cs[s, :, :] = conv_srcs[s](pl.ds(t0, gr + 16)).astype(F32)
                rows = pl.ds(pl.multiple_of(g * gr, CHUNK), gr)
            w = conv_ws[s]
            y = w[CONV_K - 1:CONV_K, :] * cs[s, 16:16 + gr, :]
            for j in range(CONV_K - 1):
                off = 16 - (CONV_K - 1) + j
                y = y + w[j:j + 1, :] * cs[s, off:off + gr, :]
            y = y * jax.nn.sigmoid(y)
            if s < 2:
                ss = jnp.sum(y * y, axis=-1, keepdims=True)
                y = y * (lax.rsqrt(ss + EPS) * (HEAD ** -0.5 if s == 0 else 1.0))
            y = y.astype(BF16)
            if s == 0:
                qs[rows, :] = y
            elif s == 1:
                ks[rows, :] = y
            else:
                vs[s - 2, rows, :] = y

    lane = lax.broadcasted_iota(jnp.int32, (CHUNK, LANES), 1)
    sub = lax.broadcasted_iota(jnp.int32, (CHUNK, LANES), 0)
    left = lane < half
    eye2 = (jnp.where(left, lane, lane - half) == sub).astype(F32)
    zero_b = jnp.zeros((CHUNK, LANES), BF16)
    zero_f = jnp.zeros((CHUNK, LANES), F32)

    def blockdiag(p):
        return jnp.concatenate([jnp.where(left, p, zero_f), jnp.where(left, zero_f, p)],
                               axis=0).astype(BF16)

    def blockdiag_wide(a, b):
        return jnp.concatenate([jnp.concatenate([a, zero_b], axis=1),
                                jnp.concatenate([zero_b, b], axis=1)], axis=0)

    st_s[...] = jnp.zeros(st_s.shape, F32)
    if ls > nc * CHUNK:
        ob[nc * CHUNK:ls, :] = jnp.zeros((ls - nc * CHUNK, 2 * LANES), F32)
    left8 = lax.broadcasted_iota(jnp.int32, (8, LANES), 1) < half

    def local_steps(i):
        cs = [i * grp + g for g in range(grp)]
        r0s = [pl.multiple_of(c * CHUNK, CHUNK) for c in cs]
        v = {}

        def gram():
            v["kb"] = [ks[pl.ds(r0, CHUNK), :] for r0 in r0s]
            v["qb"] = [qs[pl.ds(r0, CHUNK), :] for r0 in r0s]
            a2s = [lax.dot_general(jnp.concatenate([kb, qb], axis=0), jnp.concatenate([kb, kb], axis=0),
                                   NT_DIMS, preferred_element_type=F32)
                   for kb, qb in zip(v["kb"], v["qb"])]
            v["n"] = [-(a2[:CHUNK] * mb_ref[0, c, 0]) for a2, c in zip(a2s, cs)]
            v["att"] = [a2[CHUNK:] * md_ref[0, c, 0] for a2, c in zip(a2s, cs)]

        def square_first():
            v["p"] = [_dot(n.astype(BF16), blockdiag(n)) for n in v["n"]]
            v["t"] = [eye2 + n for n in v["n"]]

        def square_next():
            rs = [_dot(jnp.concatenate([p, t], axis=0).astype(BF16), blockdiag(p))
                  for p, t in zip(v["p"], v["t"])]
            v["p"] = [r[:CHUNK] for r in rs]
            v["t"] = [t + r[CHUNK:] for t, r in zip(v["t"], rs)]

        def square_last():
            v["t"] = [t + _dot(t.astype(BF16), blockdiag(p)) for t, p in zip(v["t"], v["p"])]
            v["beta"] = [rows_ref[0, 0, 0, pl.ds(c, 1), :] for c in cs]
            gc_ps = [rows_ref[0, 1, 0, pl.ds(c, 1), :] for c in cs]
            gl_ps = [rows_ref[0, 2, 0, pl.ds(c, 1), :] for c in cs]
            v["egc"] = [jnp.exp(gc_p) for gc_p in gc_ps]
            v["ekd"] = [jnp.exp(gl_p - gc_p) for gl_p, gc_p in zip(gl_ps, gc_ps)]

        def make_u():
            u2s = [_dot((t * beta_p).astype(BF16),
                        blockdiag_wide(vs[0, pl.ds(r0, CHUNK), :], vs[1, pl.ds(r0, CHUNK), :]))
                   for t, beta_p, r0 in zip(v["t"], v["beta"], r0s)]
            for c, u2 in zip(cs, u2s):
                u_s[c] = u2

        def make_w():
            v["wk"] = [_dot(jnp.concatenate([(t * (beta_p * egc)).astype(BF16),
                                             (eye2 * ekd).astype(BF16)], axis=0),
                            blockdiag_wide(kb, kb))
                       for t, beta_p, egc, ekd, kb in zip(v["t"], v["beta"], v["egc"], v["ekd"], v["kb"])]

        def make_qd():
            qd2s = [_dot((eye2 * egc).astype(BF16), blockdiag_wide(qb, qb))
                    for egc, qb in zip(v["egc"], v["qb"])]
            for g, c in enumerate(cs):
                for h in range(2):
                    sl = slice(h * LANES, (h + 1) * LANES)
                    wq_s[c, h] = jnp.concatenate([v["wk"][g][:CHUNK, sl], qd2s[g][:, sl]],
                                                 axis=0).astype(BF16)
                att_s[c] = v["att"][g].astype(BF16)
                kd_s[c] = v["wk"][g][CHUNK:].astype(BF16)

        return [gram, square_first, square_next, square_next, square_next, square_next, square_last,
                make_u, make_w, make_qd]

    def state_steps(i):
        steps = []
        for g in range(grp):
            c = i * grp + g
            v = {}

            def read_state(c=c, v=v):
                u2 = u_s[c]
                v["r"] = [_dot(wq_s[c, h], st_s[h].astype(BF16)) for h in range(2)]
                v["vn"] = [(u2[:, h * LANES:(h + 1) * LANES] - v["r"][h][:CHUNK]).astype(BF16)
                           for h in range(2)]

            def write_state(c=c, v=v):
                r0 = pl.multiple_of(c * CHUNK, CHUNK)
                egl = jnp.broadcast_to(jnp.exp(rows_ref[0, 2, 0, pl.ds(c, 1), :]), (8, LANES))
                egl_sw = pltpu.roll(egl, half, 1)
                egl_h = (jnp.where(left8, egl, egl_sw)[0:1], jnp.where(left8, egl_sw, egl)[0:1])
                o2 = _dot(att_s[c], blockdiag_wide(v["vn"][0], v["vn"][1]))
                ob[pl.ds(r0, CHUNK), :] = o2 + jnp.concatenate([v["r"][0][CHUNK:], v["r"][1][CHUNK:]],
                                                              axis=1)
                kd2 = kd_s[c]
                for h in range(2):
                    upd = lax.dot_general(kd2[:, h * LANES:(h + 1) * LANES], v["vn"][h], TN_DIMS,
                                          preferred_element_type=F32)
                    st_s[h] = st_s[h] * egl_h[h] + upd

            steps += [read_state, write_state]
        return steps

    def run_interleaved(first, second):
        n1, n2 = len(first), len(second)
        j = 0
        for k, step in enumerate(first):
            step()
            while j < n2 and (j + 1) * n1 <= (k + 1) * n2:
                second[j]()
                j += 1

    conv_group(0, True)
    if ngrp > 1:
        conv_group(jnp.int32(1), False)
    run_interleaved(local_steps(jnp.int32(0)), [])

    def body(i, carry):
        run_interleaved(state_steps(i), local_steps(i + 1))
        conv_group(jnp.minimum(i + 2, ngrp - 1), False)
        return carry

    lax.fori_loop(0, ngrp - 1, body, 0)
    run_interleaved(state_steps(jnp.int32(ngrp - 1)), [])

    onw = onw_ref[...]
    for blk in range(nblk):
        t0 = blk * rb
        for h in range(2):
            o = ob[PAD + t0:PAD + t0 + rb, h * LANES:(h + 1) * LANES]
            ms = jnp.mean(o * o, axis=-1, keepdims=True)
            zz = z_ref[h, t0:t0 + rb, :].astype(F32)
            o_ref[t0:t0 + rb, h * LANES:(h + 1) * LANES] = (
                o * lax.rsqrt(ms + EPS) * onw * (zz * jax.nn.sigmoid(zz))).astype(o_ref.dtype)


def _delta_rule(pj, cw, onw, md, mb, rows, *, bsz, lp, nc, hk_n):
    ls = max(PAD + lp, nc * CHUNK)
    rb = lp // 4 if lp % 32 == 0 and lp >= 1024 else lp
    kern = functools.partial(_delta_kernel, lp=lp, nc=nc, rb=rb)
    return pl.pallas_call(
        kern,
        grid=(bsz, hk_n),
        in_specs=[
            pl.BlockSpec((1, lp, LANES), lambda b, h: (h, b, 0)),
            pl.BlockSpec((1, lp, LANES), lambda b, h: (hk_n + h, b, 0)),
            pl.BlockSpec((2, lp, LANES), lambda b, h: (hk_n + h, b, 0)),
            pl.BlockSpec((2, lp, LANES), lambda b, h: (2 * hk_n + h, b, 0)),
            pl.BlockSpec((1, CONV_K, LANES), lambda b, h: (h, 0, 0)),
            pl.BlockSpec((1, CONV_K, LANES), lambda b, h: (hk_n + h, 0, 0)),
            pl.BlockSpec((2, CONV_K, LANES), lambda b, h: (hk_n + h, 0, 0)),
            pl.BlockSpec((1, LANES), lambda b, h: (0, 0)),
            pl.BlockSpec((1, nc, 1, CHUNK, LANES), lambda b, h: (b, 0, h, 0, 0)),
            pl.BlockSpec((1, nc, 1, CHUNK, LANES), lambda b, h: (b, 0, h, 0, 0)),
            pl.BlockSpec((1, 3, 1, _round_up(nc, 8), LANES), lambda b, h: (b, 0, h, 0, 0)),
        ],
        out_specs=pl.BlockSpec((lp, 2 * LANES), lambda b, h: (b, h)),
        out_shape=jax.ShapeDtypeStruct((bsz * lp, 2 * hk_n * LANES), BF16),
        scratch_shapes=[
            pltpu.VMEM((ls, LANES), BF16),
            pltpu.VMEM((ls, LANES), BF16),
            pltpu.VMEM((2, ls, LANES), BF16),
            pltpu.VMEM((ls, 2 * LANES), F32),
            pltpu.VMEM((nc, 2, 2 * CHUNK, LANES), BF16),
            pltpu.VMEM((nc, CHUNK, 2 * LANES), F32),
            pltpu.VMEM((nc, CHUNK, LANES), BF16),
            pltpu.VMEM((nc, CHUNK, 2 * LANES), BF16),
            pltpu.VMEM((2, HEAD, HEAD), F32),
            pltpu.VMEM((4, 16 + _unroll_for(nc) * CHUNK, LANES), F32),
        ],
        compiler_params=pltpu.CompilerParams(dimension_semantics=("parallel", "arbitrary"),
                                             vmem_limit_bytes=VMEM_LIMIT),
        name="delta_rule",
    )(pj, pj, pj, pj, cw, cw, cw, onw, md, mb, rows)


def _attn_kernel(q_ref, k_ref, v_ref, g_ref, qw_ref, kw_ref, o_ref, qn, kn, vn, acc_s, *, lp, rb, tail):
    lk = kn.shape[0]
    nblk = lp // rb
    scale = HEAD ** -0.5
    if lk > lp:
        zpad = jnp.zeros((lk - lp, HEAD), BF16)
        qn[lp:lk, :] = zpad
        kn[lp:lk, :] = zpad
        vn[lp:lk, :] = zpad
    qw = qw_ref[...]
    kw = kw_ref[...]
    for blk in range(nblk):
        t0 = blk * rb
        x = q_ref[0, t0:t0 + rb, :].astype(F32)
        ms = jnp.mean(x * x, axis=-1, keepdims=True)
        qn[t0:t0 + rb, :] = (x * lax.rsqrt(ms + EPS) * (qw * scale)).astype(BF16)
        x = k_ref[0, t0:t0 + rb, :].astype(F32)
        ms = jnp.mean(x * x, axis=-1, keepdims=True)
        kn[t0:t0 + rb, :] = (x * lax.rsqrt(ms + EPS) * kw).astype(BF16)
        vn[t0:t0 + rb, :] = v_ref[0, t0:t0 + rb, :]

    neg_upper = -(lax.broadcasted_iota(jnp.int32, (K_TILE, K_TILE), 0)
                  > lax.broadcasted_iota(jnp.int32, (K_TILE, K_TILE), 1)).astype(BF16)
    kpq = Q_TILE // K_TILE
    col_minus_row = (lax.broadcasted_iota(jnp.int32, (Q_TILE, K_TILE), 1)
                     - lax.broadcasted_iota(jnp.int32, (Q_TILE, K_TILE), 0))

    def softplus(z):
        return jnp.maximum(z, 0.0) + jnp.log(1.0 + jnp.exp(-jnp.abs(z)))

    def run_tiles(q, k0s, first_is_diagonal, acc, cr):
        z_l = [lax.dot_general(q, kn[pl.ds(k0, K_TILE), :], NT_DIMS, preferred_element_type=F32)
               for k0 in k0s]
        zs_l, spb_l, sp0_l = [], [], []
        for u, z in enumerate(z_l):
            sp = softplus(z)
            zs = z - sp
            if first_is_diagonal and u < kpq:
                causal = col_minus_row < (u + 1) * K_TILE - Q_TILE
                zs = jnp.where(causal, zs, MASKED_LOGIT)
                sp = jnp.where(causal, sp, 0.0)
            zs_l.append(zs)
            spb_l.append(sp.astype(BF16))
            sp0_l.append(sp[:, 0:1])
        tl_l = [_dot(spb, neg_upper) for spb in spb_l]
        a_l = []
        for zs, sp0, tl in zip(zs_l, sp0_l, tl_l):
            a_l.append(jnp.exp(zs + tl + cr))
            cr = cr + tl[:, 0:1] - sp0
        for a, k0 in zip(a_l, k0s):
            acc = acc + _dot(a.astype(BF16), vn[pl.ds(k0, K_TILE), :])
        return acc, cr

    def q_loop(qi, carry):
        q0 = pl.multiple_of(qi * Q_TILE, Q_TILE)
        q = qn[pl.ds(q0, Q_TILE), :]
        n_tiles = (qi + 1) * kpq
        n_loop = jnp.maximum(n_tiles // ATTN_UNROLL - 1, 0)
        n_first = n_tiles - n_loop * ATTN_UNROLL

        def first_block(r):
            def f():
                k0s = [pl.multiple_of((n_tiles - 1 - u) * K_TILE, K_TILE) for u in range(r)]
                return run_tiles(q, k0s, True, jnp.zeros((Q_TILE, HEAD), F32), jnp.zeros((Q_TILE, 1), F32))
            return f

        acc, cr = lax.switch(n_first // kpq - 1,
                             [first_block(r) for r in range(kpq, 2 * ATTN_UNROLL, kpq)])

        def k_loop(j, c2):
            base = n_tiles - 1 - n_first - j * ATTN_UNROLL
            k0s = [pl.multiple_of((base - u) * K_TILE, K_TILE) for u in range(ATTN_UNROLL)]
            return run_tiles(q, k0s, False, *c2)

        acc, _ = lax.fori_loop(0, n_loop, k_loop, (acc, cr))
        acc_s[pl.ds(q0, Q_TILE), :] = acc
        return carry

    n_full = lp // Q_TILE
    lax.fori_loop(0, n_full, q_loop, 0)

    q0t = n_full * Q_TILE
    if tail:
        nkt = lk // K_TILE
        qt = qn[q0t:q0t + tail, :]
        z = lax.dot_general(qt, kn[...], NT_DIMS, preferred_element_type=F32)
        sp = softplus(z)
        causal = (lax.broadcasted_iota(jnp.int32, (tail, lk), 1)
                  < q0t + lax.broadcasted_iota(jnp.int32, (tail, lk), 0))
        zs = jnp.where(causal, z - sp, MASKED_LOGIT)
        sp = jnp.where(causal, sp, 0.0)
        spb = sp.astype(BF16)
        tls = [_dot(spb[:, kt * K_TILE:(kt + 1) * K_TILE], neg_upper) for kt in range(nkt)]
        cr = jnp.zeros((tail, 1), F32)
        crs = [None] * nkt
        for kt in reversed(range(nkt)):
            crs[kt] = cr
            cr = cr + tls[kt][:, 0:1] - sp[:, kt * K_TILE:kt * K_TILE + 1]
        acc = jnp.zeros((tail, HEAD), F32)
        for kt in range(nkt):
            sl = slice(kt * K_TILE, (kt + 1) * K_TILE)
            a = jnp.exp(zs[:, sl] + tls[kt] + crs[kt])
            acc = acc + _dot(a.astype(BF16), vn[sl, :])
        acc_s[q0t:q0t + tail, :] = acc
    if lk > q0t + tail:
        acc_s[q0t + tail:lk, :] = jnp.zeros((lk - q0t - tail, HEAD), F32)

    for blk in range(nblk):
        t0 = blk * rb
        g = g_ref[0, t0:t0 + rb, :].astype(F32)
        o_ref[t0:t0 + rb, :] = (acc_s[t0:t0 + rb, :] * (g * jax.nn.sigmoid(g))).astype(o_ref.dtype)


def _sb_attention(pj, qw, kw, *, bsz, lp, l_real, nh):
    lk = _round_up(lp, K_TILE)
    rb = lp // 4 if lp % 64 == 0 and lp >= 1024 else lp
    tail = _round_up(max(l_real - (lp // Q_TILE) * Q_TILE, 0), 16)
    return pl.pallas_call(
        functools.partial(_attn_kernel, lp=lp, rb=rb, tail=tail),
        grid=(bsz, nh),
        in_specs=[
            pl.BlockSpec((1, lp, HEAD), lambda b, h: (h, b, 0)),
            pl.BlockSpec((1, lp, HEAD), lambda b, h: (nh + h, b, 0)),
            pl.BlockSpec((1, lp, HEAD), lambda b, h: (2 * nh + h, b, 0)),
            pl.BlockSpec((1, lp, HEAD), lambda b, h: (3 * nh + h, b, 0)),
            pl.BlockSpec((1, HEAD), lambda b, h: (0, 0)),
            pl.BlockSpec((1, HEAD), lambda b, h: (0, 0)),
        ],
        out_specs=pl.BlockSpec((lp, HEAD), lambda b, h: (b, h)),
        out_shape=jax.ShapeDtypeStruct((bsz * lp, nh * HEAD), BF16),
        scratch_shapes=[pltpu.VMEM((lk, HEAD), BF16),
                        pltpu.VMEM((lk, HEAD), BF16),
                        pltpu.VMEM((lk, HEAD), BF16),
                        pltpu.VMEM((lk, HEAD), F32)],
        compiler_params=pltpu.CompilerParams(dimension_semantics=("parallel", "parallel"),
                                             vmem_limit_bytes=VMEM_LIMIT),
        name="sb_attention",
    )(pj, pj, pj, pj, qw, kw)


def _pick_tile(total, prefs):
    for t in prefs:
        if total % t == 0:
            return t
    return total


def kernel(x, meta_tokens, dn_norm_w, dn_w_in, dn_conv_w, dn_a_log, dn_dt_bias, dn_out_norm_w, dn_w_out,
           sb_norm_w, sb_w_in, sb_q_norm_w, sb_k_norm_w, sb_w_out):
    bsz, seq, d = x.shape
    hk_n = d // HEAD
    hv_n = 2 * hk_n
    assert hv_n == LANES // 2, "gate layout assumes 64 value heads"
    l_real = N_META + seq
    lp = _round_up(l_real, LANES)
    nc = -(-(PAD + l_real) // CHUNK)
    m = bsz * lp
    tm = _pick_tile(m, (768, 512, 384, 256, 128))
    tn_rms = _pick_tile(m, (256, 128))

    meta = jnp.broadcast_to(meta_tokens[None].astype(x.dtype), (bsz, N_META, d))
    h0 = jnp.concatenate([meta, x, jnp.zeros((bsz, lp - l_real, d), x.dtype)], axis=1).reshape(m, d)

    n_main = 2 * hk_n * HEAD + 2 * hv_n * HEAD
    perm = jnp.concatenate([jnp.arange(0, hv_n, 2), jnp.arange(1, hv_n, 2)])
    perm = jnp.concatenate([perm, hv_n + perm])
    w_ba = dn_w_in[0, :, n_main:][:, perm].astype(BF16)
    half_perm = perm[:hv_n]
    zeros64 = jnp.zeros((hv_n,), F32)
    alog = jnp.concatenate([zeros64, dn_a_log[0][half_perm].astype(F32)]).reshape(1, LANES)
    dtb = jnp.concatenate([zeros64, dn_dt_bias[0][half_perm].astype(F32)]).reshape(1, LANES)

    hn = _rmsnorm(h0, dn_norm_w[0], tn_rms)
    pj0 = _matmul_wcast(hn, dn_w_in, n_main, tm=tm, tn=_pick_tile(n_main, (768, 512, 256)), name="dn_in_proj")
    ba = _matmul(hn, w_ba, tm=tm, tn=LANES, tk=d, name="dn_gate_proj")
    md, mb, rows = _gate_prep(ba, alog, dtb, bsz=bsz, lp=lp, nc=nc)
    cw = dn_conv_w[0].astype(F32).reshape(CONV_K, -1, LANES).transpose(1, 0, 2)
    o0 = _delta_rule(pj0, cw, dn_out_norm_w[0].astype(F32).reshape(1, LANES), md, mb, rows,
                     bsz=bsz, lp=lp, nc=nc, hk_n=hk_n)
    h1 = _matmul(o0, dn_w_out[0].astype(BF16), tm=tm, tn=512, tk=2 * d, res=h0, name="dn_out_proj")

    hn1 = _rmsnorm(h1, sb_norm_w[0], tn_rms)
    pj1 = _matmul_wcast(hn1, sb_w_in, sb_w_in.shape[-1], tm=tm, tn=512, name="sb_in_proj")
    o1 = _sb_attention(pj1, sb_q_norm_w[0].astype(F32).reshape(1, HEAD),
                       sb_k_norm_w[0].astype(F32).reshape(1, HEAD), bsz=bsz, lp=lp, l_real=l_real, nh=hk_n)
    h2 = _matmul(o1, sb_w_out[0].astype(BF16), tm=tm, tn=1024, tk=d, res=h1, name="sb_out_proj")
    return h2.reshape(bsz, lp, d)[:, N_META:l_real]
```
